```python
import math
import jax
import jax.numpy as jnp
from jax import lax
import numpy as np

D_MODEL = 1024
BATCH = 8
SEQ = 8192
DEPTH = 2

CHUNK = 64
Q_BLOCK = 128
HEAD_DIM = 64
DIFF_WIDTH = D_MODEL // 2
N_DIFF_HEADS = DIFF_WIDTH // (2 * HEAD_DIM)
SB_WIDTH = D_MODEL - DIFF_WIDTH
N_SB_HEADS = SB_WIDTH // HEAD_DIM
PROJ_WIDTH = 3 * DIFF_WIDTH + 3 * SB_WIDTH
N_BUCKETS = 32
MAX_DISTANCE = 128
N_EXPERTS = 32
TOP_K = 4
D_FF = D_MODEL
SWIGLU_LIMIT = 7.0
SWIGLU_ALPHA = 1.702
MOE_BLOCK = 256
NORM_EPS = 1e-5
N_MOD = 6

kernel_name = "hybrid_diffattn_stickbreak_moe_encoder"


def rms_norm(x, gain):
    xf = x.astype(jnp.float32)
    y = xf * lax.rsqrt(jnp.mean(xf * xf, axis=-1, keepdims=True) + NORM_EPS)
    return (y * gain.astype(jnp.float32)).astype(x.dtype)


def t5_bucket(rel):
    half = N_BUCKETS // 2
    max_exact = half // 2
    n = jnp.abs(rel)
    nf = jnp.maximum(n, 1).astype(jnp.float32)
    large = max_exact + (jnp.log(nf / max_exact) / math.log(MAX_DISTANCE / max_exact)
                         * (half - max_exact)).astype(jnp.int32)
    large = jnp.minimum(large, half - 1)
    return jnp.where(rel > 0, half, 0) + jnp.where(n < max_exact, n, large)


def diff_attention(q, k, v, lam, lam_init, subln, rel_bias):
    b, s, h, _, dh = q.shape
    scale = dh ** -0.5
    outs = []
    for i in range(s // Q_BLOCK):
        q0, q1 = i * Q_BLOCK, (i + 1) * Q_BLOCK
        qpos = jnp.arange(q0, q1)
        kpos = jnp.arange(q1)
        logits = jnp.einsum("bqhmd,bkhmd->bhmqk", q[:, q0:q1], k[:, :q1]).astype(jnp.float32) * scale
        bias = rel_bias[t5_bucket(kpos[None, :] - qpos[:, None])].astype(jnp.float32)
        logits = logits + jnp.transpose(bias, (2, 0, 1))[None, :, None]
        mask = (kpos[None, :] // CHUNK) <= (qpos[:, None] // CHUNK)
        p = jax.nn.softmax(jnp.where(mask, logits, -jnp.inf), axis=-1)
        a = p[:, :, 0] - lam * p[:, :, 1]
        outs.append(jnp.einsum("bhqk,bkhe->bqhe", a.astype(v.dtype), v[:, :q1]))
    o = jnp.concatenate(outs, axis=1)
    o = rms_norm(o, subln) * (1.0 - lam_init)
    return o.reshape(b, s, h * 2 * dh)


def stick_breaking(q, k, v, gain):
    b, s, h, dh = q.shape
    scale = dh ** -0.5
    outs = []
    for i in range(s // Q_BLOCK):
        q0, q1 = i * Q_BLOCK, (i + 1) * Q_BLOCK
        qpos = jnp.arange(q0, q1)
        kpos = jnp.arange(q1)
        z = jnp.einsum("bqhd,bkhd->bhqk", q[:, q0:q1], k[:, :q1]).astype(jnp.float32) * scale
        strict = kpos[None, :] < qpos[:, None]
        log_fail = jnp.where(strict, jax.nn.log_sigmoid(-z), 0.0)
        after = lax.cumsum(log_fail, axis=3, reverse=True) - log_fail
        a = jnp.where(strict, jnp.exp(jax.nn.log_sigmoid(z) + after), 0.0)
        outs.append(jnp.einsum("bhqk,bkhd->bqhd", a.astype(v.dtype), v[:, :q1]))
    o = jnp.concatenate(outs, axis=1)
    o = rms_norm(o, gain)
    return o.reshape(b, s, h * dh)


def moe(h, w_router, b_router, w_gate_up, b_gate_up, w_down, b_down):
    b, s, d = h.shape
    n_tok = b * s
    n_rows = n_tok * TOP_K
    xt = h.reshape(n_tok, d)
    logits = (xt @ w_router + b_router).astype(jnp.float32)
    top_val, top_idx = lax.top_k(logits, TOP_K)
    gates = jax.nn.softmax(top_val, axis=-1).astype(h.dtype)
    flat_e = top_idx.reshape(-1)
    order = jnp.argsort(flat_e, stable=True)
    sorted_e = flat_e[order]
    token_of = (order // TOP_K).astype(jnp.int32)
    counts = jnp.bincount(flat_e, length=N_EXPERTS)
    start = jnp.cumsum(counts) - counts
    padded = (counts + MOE_BLOCK - 1) // MOE_BLOCK * MOE_BLOCK
    pend = jnp.cumsum(padded)
    pstart = pend - padded
    dest = pstart[sorted_e] + (jnp.arange(n_rows) - start[sorted_e])
    n_blocks = -(-(n_rows + N_EXPERTS * (MOE_BLOCK - 1)) // MOE_BLOCK)
    n_slots = n_blocks * MOE_BLOCK
    slot_tok = jnp.zeros((n_slots,), jnp.int32).at[dest].set(token_of)
    slot_gate = jnp.zeros((n_slots,), h.dtype).at[dest].set(gates.reshape(-1)[order])
    block_e = jnp.minimum(jnp.searchsorted(pend, jnp.arange(n_blocks) * MOE_BLOCK, side="right"),
                          N_EXPERTS - 1).astype(jnp.int32)

    def expert_rows(args):
        tok, g, e = args
        xb = xt[tok]
        gu = xb @ w_gate_up[e] + b_gate_up[e]
        glu = jnp.minimum(gu[:, :D_FF], SWIGLU_LIMIT)
        lin = jnp.clip(gu[:, D_FF:], -SWIGLU_LIMIT, SWIGLU_LIMIT)
        act = glu * jax.nn.sigmoid(SWIGLU_ALPHA * glu) * (lin + 1.0)
        return (act @ w_down[e] + b_down[e]) * g[:, None]

    y = lax.map(expert_rows, (slot_tok.reshape(n_blocks, MOE_BLOCK),
                              slot_gate.reshape(n_blocks, MOE_BLOCK), block_e))
    out = jnp.zeros((n_tok, d), y.dtype).at[slot_tok].add(y.reshape(n_slots, d))
    return out.reshape(b, s, d)


def setup_inputs(seed: int = 0) -> dict:
    key = jax.random.key(seed)
    ks = jax.random.split(key, 19)

    def nrm(k, shape, scale):
        return jax.random.normal(k, shape, jnp.float32) * scale

    return {
        "x": nrm(ks[0], (BATCH, SEQ, D_MODEL), 1.0),
        "c": nrm(ks[1], (BATCH, D_MODEL), 1.0),
        "w_in": nrm(ks[2], (DEPTH, D_MODEL, PROJ_WIDTH), D_MODEL ** -0.5),
        "w_out": nrm(ks[3], (DEPTH, D_MODEL, D_MODEL), D_MODEL ** -0.5),
        "norm_mix": 1.0 + nrm(ks[4], (DEPTH, D_MODEL), 0.02),
        "norm_ffn": 1.0 + nrm(ks[5], (DEPTH, D_MODEL), 0.02),
        "w_ada": nrm(ks[6], (DEPTH, D_MODEL, N_MOD * D_MODEL), 0.5 * D_MODEL ** -0.5),
        "b_ada": nrm(ks[7], (DEPTH, N_MOD * D_MODEL), 0.02),
        "lam_qk": nrm(ks[8], (DEPTH, 4, HEAD_DIM), 0.1),
        "diff_subln": 1.0 + nrm(ks[9], (DEPTH, 2 * HEAD_DIM), 0.02),
        "sb_norm": 1.0 + nrm(ks[10], (DEPTH, HEAD_DIM), 0.02),
        "rel_bias": nrm(ks[11], (N_BUCKETS, N_DIFF_HEADS), 0.5),
        "w_router": nrm(ks[12], (DEPTH, D_MODEL, N_EXPERTS), D_MODEL ** -0.5),
        "b_router": nrm(ks[13], (DEPTH, N_EXPERTS), 0.01),
        "w_gate_up": nrm(ks[14], (DEPTH, N_EXPERTS, D_MODEL, 2 * D_FF), D_MODEL ** -0.5),
        "b_gate_up": nrm(ks[15], (DEPTH, N_EXPERTS, 2 * D_FF), 0.02),
        "w_down": nrm(ks[16], (DEPTH, N_EXPERTS, D_FF, D_MODEL), D_FF ** -0.5),
        "b_down": nrm(ks[17], (DEPTH, N_EXPERTS, D_MODEL), 0.02),
        "final_norm": 1.0 + nrm(ks[18], (D_MODEL,), 0.02),
    }


def reference(x, c, w_in, w_out, norm_mix, norm_ffn, w_ada, b_ada, lam_qk, diff_subln, sb_norm,
              rel_bias, w_router, b_router, w_gate_up, b_gate_up, w_down, b_down, final_norm):
    b, s, _ = x.shape
    cond = jax.nn.silu(c)
    splits = [DIFF_WIDTH, 2 * DIFF_WIDTH, 3 * DIFF_WIDTH,
              3 * DIFF_WIDTH + SB_WIDTH, 3 * DIFF_WIDTH + 2 * SB_WIDTH]
    for l in range(DEPTH):
        mod = (cond @ w_ada[l] + b_ada[l])[:, None, :]
        shift1, scale1, gate1, shift2, scale2, gate2 = jnp.split(mod, N_MOD, axis=-1)

        hmix = rms_norm(x, norm_mix[l]) * (1.0 + scale1) + shift1
        proj = hmix @ w_in[l]
        dq, dk, dv, sq, sk, sv = jnp.split(proj, splits, axis=-1)
        lam_init = 0.8 - 0.6 * math.exp(-0.3 * l)
        lq = lam_qk[l].astype(jnp.float32)
        lam = jnp.exp(jnp.sum(lq[0] * lq[1])) - jnp.exp(jnp.sum(lq[2] * lq[3])) + lam_init
        o_diff = diff_attention(dq.reshape(b, s, N_DIFF_HEADS, 2, HEAD_DIM),
                                dk.reshape(b, s, N_DIFF_HEADS, 2, HEAD_DIM),
                                dv.reshape(b, s, N_DIFF_HEADS, 2 * HEAD_DIM),
                                lam, lam_init, diff_subln[l], rel_bias)
        o_sb = stick_breaking(sq.reshape(b, s, N_SB_HEADS, HEAD_DIM),
                              sk.reshape(b, s, N_SB_HEADS, HEAD_DIM),
                              sv.reshape(b, s, N_SB_HEADS, HEAD_DIM), sb_norm[l])
        x = x + gate1 * (jnp.concatenate([o_diff, o_sb], axis=-1) @ w_out[l])

        hffn = rms_norm(x, norm_ffn[l]) * (1.0 + scale2) + shift2
        x = x + gate2 * moe(hffn, w_router[l], b_router[l], w_gate_up[l], b_gate_up[l],
                            w_down[l], b_down[l])
    return rms_norm(x, final_norm)
```

```python
import functools
import math

import jax
import jax.numpy as jnp
from jax import lax
from jax.experimental import pallas as pl
from jax.experimental.pallas import tpu as pltpu

D_MODEL = 1024
HEAD_DIM = 64
CHUNK = 64
DIFF_WIDTH = 512
N_DIFF_HEADS = 4
SB_WIDTH = 512
N_SB_HEADS = 8
PROJ_WIDTH = 3 * DIFF_WIDTH + 3 * SB_WIDTH
N_BUCKETS = 32
MAX_DISTANCE = 128
N_EXPERTS = 32
TOP_K = 4
D_FF = D_MODEL
SWIGLU_LIMIT = 7.0
SWIGLU_ALPHA = 1.702
MOE_BLOCK = 256
NORM_EPS = 1e-5
N_MOD = 6

LANES = 128
N_COL_BLOCKS = PROJ_WIDTH // LANES
ATT_TQ = 256
ATT_TK = 256
ROW_TILE = 512
MASK_VALUE = -1e30
VMEM_LIMIT = 56 * 1024 * 1024

_NT_DIMS = (((1,), (1,)), ((), ()))


def _cparams(sem):
    return pltpu.CompilerParams(dimension_semantics=sem, vmem_limit_bytes=VMEM_LIMIT)


def _ada_kernel(c_ref, w_ref, b_ref, o_ref):
    c = c_ref[...]
    cond = c * jax.nn.sigmoid(c)
    o_ref[0] = jnp.dot(cond, w_ref[0], preferred_element_type=jnp.float32,
                       precision=lax.Precision.HIGHEST) + b_ref[0]


def _ada_mod(c, w_ada, b_ada):
    depth, d, n = w_ada.shape
    b = c.shape[0]
    tn = 1536
    return pl.pallas_call(
        _ada_kernel,
        grid=(depth, n // tn),
        in_specs=[pl.BlockSpec((b, d), lambda l, j: (0, 0)),
                  pl.BlockSpec((1, d, tn), lambda l, j: (l, 0, j)),
                  pl.BlockSpec((1, 1, tn), lambda l, j: (l, 0, j))],
        out_specs=pl.BlockSpec((1, b, tn), lambda l, j: (l, 0, j)),
        out_shape=jax.ShapeDtypeStruct((depth, b, n), jnp.float32),
        compiler_params=_cparams(("arbitrary", "arbitrary")),
        name="ada_mod",
    )(c, w_ada, b_ada.reshape(depth, 1, n))


def _inproj_kernel(x_ref, mod_ref, g_ref, w_ref, cs_ref, o_ref):
    x = x_ref[0]
    ms = jnp.mean(x * x, axis=-1, keepdims=True)
    y = x * lax.rsqrt(ms + NORM_EPS) * g_ref[...]
    h = y * (1.0 + mod_ref[0, 1:2, :]) + mod_ref[0, 0:1, :]
    hb = h.astype(jnp.bfloat16)
    nchunk = 512
    for c in range(PROJ_WIDTH // nchunk):
        acc = jnp.dot(hb, w_ref[:, c * nchunk:(c + 1) * nchunk], preferred_element_type=jnp.float32)
        acc = acc * cs_ref[:, c * nchunk:(c + 1) * nchunk]
        for j in range(nchunk // LANES):
            o_ref[0, c * (nchunk // LANES) + j] = acc[:, j * LANES:(j + 1) * LANES].astype(jnp.bfloat16)


def _inproj(x, mod, g, w_bf, colscale):
    b, s, d = x.shape
    tm = min(ROW_TILE, s)
    return pl.pallas_call(
        _inproj_kernel,
        grid=(b, s // tm),
        in_specs=[pl.BlockSpec((1, tm, d), lambda bi, i: (bi, i, 0)),
                  pl.BlockSpec((1, N_MOD, d), lambda bi, i: (bi, 0, 0)),
                  pl.BlockSpec((1, d), lambda bi, i: (0, 0)),
                  pl.BlockSpec((d, PROJ_WIDTH), lambda bi, i: (0, 0)),
                  pl.BlockSpec((1, PROJ_WIDTH), lambda bi, i: (0, 0))],
        out_specs=pl.BlockSpec((1, N_COL_BLOCKS, tm, LANES), lambda bi, i: (bi, 0, i, 0)),
        out_shape=jax.ShapeDtypeStruct((b, N_COL_BLOCKS, s, LANES), jnp.bfloat16),
        compiler_params=_cparams(("arbitrary", "arbitrary")),
        name="inproj",
    )(x, mod, g, w_bf, colscale)


def _split_halves(q):
    lane = lax.broadcasted_iota(jnp.int32, q.shape, 1)
    zero = jnp.zeros_like(q)
    return jnp.concatenate([jnp.where(lane < HEAD_DIM, q, zero), jnp.where(lane >= HEAD_DIM, q, zero)], axis=0)


def _lane_tile(x, n):
    return x if n == 1 else jnp.concatenate([x] * n, axis=1)


def _diff_kernel(q_ref, k_ref, v_ref, bias_ref, lam_ref, g_ref, o_ref, m_ref, l_ref, acc_ref, *, out_scale):
    i = pl.program_id(2)
    tq, tk = ATT_TQ, ATT_TK
    q2 = _split_halves(q_ref[0, 0])

    m_ref[...] = jnp.full(m_ref.shape, -jnp.inf, jnp.float32)
    l_ref[...] = jnp.zeros(l_ref.shape, jnp.float32)
    acc_ref[...] = jnp.zeros(acc_ref.shape, jnp.float32)

    def step(j, bias):
        off = pl.multiple_of(j * tk, tk)
        k = k_ref[0, 0, pl.ds(off, tk), :]
        v = v_ref[0, 0, pl.ds(off, tk), :]
        s = lax.dot_general(q2, k, _NT_DIMS, preferred_element_type=jnp.float32)
        if bias is not None:
            s = s + jnp.concatenate([bias, bias], axis=0)
        m_prev = m_ref[...]
        m_next = jnp.maximum(m_prev, jnp.max(s, axis=1, keepdims=True))
        p = jnp.exp(s - _lane_tile(m_next, tk // LANES))
        alpha = jnp.exp(m_prev - m_next)
        l_ref[...] = alpha * l_ref[...] + jnp.sum(p, axis=1, keepdims=True)
        acc_ref[...] = alpha * acc_ref[...] + jnp.dot(p.astype(jnp.bfloat16), v,
                                                      preferred_element_type=jnp.float32)
        m_ref[...] = m_next

    step(i, bias_ref[0, :, tk:2 * tk])

    @pl.when(i >= 1)
    def _():
        step(i - 1, bias_ref[0, :, 0:tk])

    def far(j, carry):
        step(j, None)
        return carry

    lax.fori_loop(0, jnp.maximum(i - 1, 0), far, 0)

    inv_l = 1.0 / l_ref[...]
    o = acc_ref[...] * inv_l
    od = o[:tq] - lam_ref[...] * o[tq:]
    ms = jnp.mean(od * od, axis=-1, keepdims=True)
    y = od * lax.rsqrt(ms + NORM_EPS) * g_ref[...] * out_scale
    o_ref[0] = y.astype(jnp.bfloat16)


def _diff_attention(proj, bias, lamv, subln, out_scale):
    b, _, s, _ = proj.shape
    tq = ATT_TQ
    kern = functools.partial(_diff_kernel, out_scale=out_scale)
    return pl.pallas_call(
        kern,
        grid=(b, N_DIFF_HEADS, s // tq),
        in_specs=[pl.BlockSpec((1, 1, tq, LANES), lambda bi, h, i: (bi, h, i, 0)),
                  pl.BlockSpec((1, 1, s, LANES), lambda bi, h, i: (bi, N_DIFF_HEADS + h, 0, 0)),
                  pl.BlockSpec((1, 1, s, LANES), lambda bi, h, i: (bi, 2 * N_DIFF_HEADS + h, 0, 0)),
                  pl.BlockSpec((1, tq, 2 * ATT_TK), lambda bi, h, i: (h, 0, 0)),
                  pl.BlockSpec((1, LANES), lambda bi, h, i: (0, 0)),
                  pl.BlockSpec((1, LANES), lambda bi, h, i: (0, 0))],
        out_specs=pl.BlockSpec((1, tq, LANES), lambda bi, h, i: (bi, i, h)),
        out_shape=jax.ShapeDtypeStruct((b, s, DIFF_WIDTH), jnp.bfloat16),
        scratch_shapes=[pltpu.VMEM((2 * tq, LANES), jnp.float32),
                        pltpu.VMEM((2 * tq, LANES), jnp.float32),
                        pltpu.VMEM((2 * tq, LANES), jnp.float32)],
        compiler_params=_cparams(("arbitrary", "arbitrary", "arbitrary")),
        name="diff_attn",
    )(proj, proj, proj, bias, lamv, subln)


def _t5_bucket(rel):
    half = N_BUCKETS // 2
    max_exact = half // 2
    n = jnp.abs(rel)
    nf = jnp.maximum(n, 1).astype(jnp.float32)
    large = max_exact + (jnp.log(nf / max_exact) / math.log(MAX_DISTANCE / max_exact)
                         * (half - max_exact)).astype(jnp.int32)
    large = jnp.minimum(large, half - 1)
    return jnp.where(rel > 0, half, 0) + jnp.where(n < max_exact, n, large)


def _diff_bias_tiles(rel_bias):
    tq, tk = ATT_TQ, ATT_TK
    qpos = jnp.arange(tq)[:, None]
    kpos = jnp.arange(-tk, tk)[None, :]
    bucket = _t5_bucket(kpos - qpos)
    far_bucket = _t5_bucket(jnp.full((1, 1), -(tk + 1), jnp.int32))[0, 0]
    bias = rel_bias.astype(jnp.float32)[bucket] - rel_bias.astype(jnp.float32)[far_bucket]
    allowed = (kpos // CHUNK) <= (qpos // CHUNK)
    bias = jnp.where(allowed[:, :, None], bias, MASK_VALUE)
    return jnp.transpose(bias, (2, 0, 1))


def _sb_kernel(q_ref, k_ref, v_ref, tri_ref, g_ref, o_ref, carry_ref, acc_ref):
    i = pl.program_id(2)
    tq, tk = ATT_TQ, ATT_TK
    q2 = _split_halves(q_ref[0, 0])
    tri = tri_ref[...]

    carry_ref[...] = jnp.zeros(carry_ref.shape, jnp.float32)
    acc_ref[...] = jnp.zeros(acc_ref.shape, jnp.float32)

    def step(j, diagonal):
        off = pl.multiple_of(j * tk, tk)
        k = k_ref[0, 0, pl.ds(off, tk), :]
        v = v_ref[0, 0, pl.ds(off, tk), :]
        zn = lax.dot_general(q2, k, _NT_DIMS, preferred_element_type=jnp.float32)
        lf = jnp.minimum(zn, 0.0) - jnp.log(1.0 + jnp.exp(-jnp.abs(zn)))
        if diagonal:
            row = lax.broadcasted_iota(jnp.int32, (tq, tk), 0)
            col = lax.broadcasted_iota(jnp.int32, (tq, tk), 1)
            strict = jnp.concatenate([col < row, col < row], axis=0)
            lf = jnp.where(strict, lf, 0.0)
        cum = jnp.dot(lf.astype(jnp.bfloat16), tri, preferred_element_type=jnp.float32)
        a = jnp.exp(cum - zn + _lane_tile(carry_ref[...], tk // LANES))
        if diagonal:
            a = jnp.where(strict, a, 0.0)
        acc_ref[...] += jnp.dot(a.astype(jnp.bfloat16), v, preferred_element_type=jnp.float32)
        carry_ref[...] += jnp.sum(lf, axis=1, keepdims=True)

    step(i, True)

    def far(t, c):
        step(i - 1 - t, False)
        return c

    lax.fori_loop(0, i, far, 0)

    acc = acc_ref[...]
    lane = lax.broadcasted_iota(jnp.int32, (tq, LANES), 1)
    first = lane < HEAD_DIM
    o = jnp.where(first, acc[:tq], acc[tq:])
    sq = o * o
    ss_a = jnp.sum(jnp.where(first, sq, 0.0), axis=-1, keepdims=True)
    ss_b = jnp.sum(jnp.where(first, 0.0, sq), axis=-1, keepdims=True)
    ms = jnp.where(first, ss_a, ss_b) * (1.0 / HEAD_DIM)
    o_ref[0] = (o * lax.rsqrt(ms + NORM_EPS) * g_ref[...]).astype(jnp.bfloat16)


def _sb_attention(proj, tri, gain2):
    b, _, s, _ = proj.shape
    tq = ATT_TQ
    npair = N_SB_HEADS // 2
    base = 3 * N_DIFF_HEADS
    return pl.pallas_call(
        _sb_kernel,
        grid=(b, npair, s // tq),
        in_specs=[pl.BlockSpec((1, 1, tq, LANES), lambda bi, h, i: (bi, base + h, i, 0)),
                  pl.BlockSpec((1, 1, s, LANES), lambda bi, h, i: (bi, base + npair + h, 0, 0)),
                  pl.BlockSpec((1, 1, s, LANES), lambda bi, h, i: (bi, base + 2 * npair + h, 0, 0)),
                  pl.BlockSpec((ATT_TK, ATT_TK), lambda bi, h, i: (0, 0)),
                  pl.BlockSpec((1, LANES), lambda bi, h, i: (0, 0))],
        out_specs=pl.BlockSpec((1, tq, LANES), lambda bi, h, i: (bi, i, h)),
        out_shape=jax.ShapeDtypeStruct((b, s, SB_WIDTH), jnp.bfloat16),
        scratch_shapes=[pltpu.VMEM((2 * tq, LANES), jnp.float32),
                        pltpu.VMEM((2 * tq, LANES), jnp.float32)],
        compiler_params=_cparams(("arbitrary", "arbitrary", "arbitrary")),
        name="sb_attn",
    )(proj, proj, proj, tri, gain2)


def _outproj_kernel(x_ref, od_ref, os_ref, mod_ref, g_ref, wo_ref, wr_ref, br_ref,
                    x1_ref, h_ref, idx_ref, gate_ref):
    y = jnp.dot(od_ref[...], wo_ref[0:DIFF_WIDTH, :], preferred_element_type=jnp.float32)
    y = y + jnp.dot(os_ref[...], wo_ref[DIFF_WIDTH:, :], preferred_element_type=jnp.float32)
    x1 = x_ref[...] + mod_ref[0, 2:3, :] * y
    x1_ref[...] = x1
    ms = jnp.mean(x1 * x1, axis=-1, keepdims=True)
    h = x1 * lax.rsqrt(ms + NORM_EPS) * g_ref[...]
    h = h * (1.0 + mod_ref[0, 4:5, :]) + mod_ref[0, 3:4, :]
    h_ref[...] = h
    logits = lax.dot_general(wr_ref[...], h, _NT_DIMS, preferred_element_type=jnp.float32,
                             precision=lax.Precision.HIGHEST) + br_ref[...]
    eidx = lax.broadcasted_iota(jnp.int32, logits.shape, 0)
    vals, idxs = [], []
    for _ in range(TOP_K):
        mx = jnp.max(logits, axis=0, keepdims=True)
        sel = jnp.min(jnp.where(logits == mx, eidx, N_EXPERTS), axis=0, keepdims=True)
        vals.append(mx)
        idxs.append(sel)
        logits = jnp.where(eidx == sel, -jnp.inf, logits)
    ex = [jnp.exp(v - vals[0]) for v in vals]
    denom = ex[0] + ex[1] + ex[2] + ex[3]
    idx_ref[...] = jnp.concatenate(idxs, axis=0)
    gate_ref[...] = jnp.concatenate([e / denom for e in ex], axis=0)


def _outproj_router(x2d, od, osb, mod, g, wo_bf, wr_t, br, seq):
    n, d = x2d.shape
    tm = min(ROW_TILE, seq)
    per_b = seq // tm
    return pl.pallas_call(
        _outproj_kernel,
        grid=(n // tm,),
        in_specs=[pl.BlockSpec((tm, d), lambda i: (i, 0)),
                  pl.BlockSpec((tm, DIFF_WIDTH), lambda i: (i, 0)),
                  pl.BlockSpec((tm, SB_WIDTH), lambda i: (i, 0)),
                  pl.BlockSpec((1, N_MOD, d), lambda i: (i // per_b, 0, 0)),
                  pl.BlockSpec((1, d), lambda i: (0, 0)),
                  pl.BlockSpec((d, d), lambda i: (0, 0)),
                  pl.BlockSpec((N_EXPERTS, d), lambda i: (0, 0)),
                  pl.BlockSpec((N_EXPERTS, 1), lambda i: (0, 0))],
        out_specs=[pl.BlockSpec((tm, d), lambda i: (i, 0)),
                   pl.BlockSpec((tm, d), lambda i: (i, 0)),
                   pl.BlockSpec((TOP_K, tm), lambda i: (0, i)),
                   pl.BlockSpec((TOP_K, tm), lambda i: (0, i))],
        out_shape=[jax.ShapeDtypeStruct((n, d), jnp.float32),
                   jax.ShapeDtypeStruct((n, d), jnp.float32),
                   jax.ShapeDtypeStruct((TOP_K, n), jnp.int32),
                   jax.ShapeDtypeStruct((TOP_K, n), jnp.float32)],
        compiler_params=_cparams(("arbitrary",)),
        name="outproj_router",
    )(x2d, od, osb, mod, g, wo_bf, wr_t, br)


def _moe_n_blocks(n_tok):
    n_rows = n_tok * TOP_K
    return -(-(n_rows + N_EXPERTS * (MOE_BLOCK - 1)) // MOE_BLOCK)


def _routing_tables(top_idx, gates):
    n_tok = top_idx.shape[1]
    n_rows = n_tok * TOP_K
    n_blocks = _moe_n_blocks(n_tok)
    n_slots = n_blocks * MOE_BLOCK
    f = (jnp.arange(n_tok, dtype=jnp.int32)[None, :] * TOP_K + jnp.arange(TOP_K, dtype=jnp.int32)[:, None])
    keys = (top_idx * n_rows + f).reshape(-1)
    skeys, sgates = lax.sort((keys, gates.reshape(-1)), num_keys=1)
    starts = jnp.searchsorted(skeys, jnp.arange(N_EXPERTS + 1, dtype=jnp.int32) * n_rows).astype(jnp.int32)
    counts = starts[1:] - starts[:-1]
    padded = (counts + MOE_BLOCK - 1) // MOE_BLOCK * MOE_BLOCK
    pend = jnp.cumsum(padded)
    pstart = pend - padded
    block_e = jnp.minimum(jnp.searchsorted(pend, jnp.arange(n_blocks, dtype=jnp.int32) * MOE_BLOCK, side="right"),
                          N_EXPERTS - 1).astype(jnp.int32)
    slot = jnp.arange(n_slots, dtype=jnp.int32)
    e = jnp.repeat(block_e, MOE_BLOCK)
    r = slot - pstart[e]
    valid = r < counts[e]
    src = jnp.clip(starts[e] + r, 0, n_rows - 1)
    spare = n_rows + slot % (2 * MOE_BLOCK)
    slot_f = jnp.where(valid, skeys[src] % n_rows, spare).astype(jnp.int32)
    slot_gate = jnp.where(valid, sgates[src], 0.0)
    return block_e, slot_f.reshape(n_blocks, 1, MOE_BLOCK), slot_gate.reshape(n_slots, 1)


def _moe_kernel(be_ref, sf_ref, sfn_ref, h_hbm, g_ref, wgu_ref, bgu_ref, wd_ref, bd_ref, y_hbm,
                xbuf, ybuf, gsem, ssem, *, n_rows):
    i = pl.program_id(0)
    nb = pl.num_programs(0)
    slot = i % 2
    nslot = 1 - slot

    def gather_copy(tok, buf_slot, r):
        return pltpu.make_async_copy(h_hbm.at[pl.ds(tok, 1)], xbuf.at[buf_slot, pl.ds(r, 1)], gsem.at[buf_slot])

    def scatter_copy(f, buf_slot, r):
        return pltpu.make_async_copy(ybuf.at[buf_slot, pl.ds(r, 1)], y_hbm.at[pl.ds(f, 1)], ssem.at[buf_slot])

    def start_gather(idx_ref, buf_slot):
        def body(r, c):
            f = idx_ref[0, 0, r]
            tok = jnp.where(f < n_rows, f // TOP_K, 0)
            gather_copy(tok, buf_slot, r).start()
            return c
        lax.fori_loop(0, MOE_BLOCK, body, 0)

    def wait_rows(make, buf_slot):
        def body(r, c):
            make(0, buf_slot, r).wait()
            return c
        lax.fori_loop(0, MOE_BLOCK, body, 0)

    @pl.when(i == 0)
    def _():
        start_gather(sf_ref, slot)

    @pl.when(i + 1 < nb)
    def _():
        start_gather(sfn_ref, nslot)

    wait_rows(gather_copy, slot)

    xb = xbuf[slot].astype(jnp.bfloat16)
    gu = jnp.dot(xb, wgu_ref[0], preferred_element_type=jnp.float32) + bgu_ref[0]
    glu = jnp.minimum(gu[:, :D_FF], SWIGLU_LIMIT)
    lin = jnp.clip(gu[:, D_FF:], -SWIGLU_LIMIT, SWIGLU_LIMIT)
    act = glu * jax.nn.sigmoid(SWIGLU_ALPHA * glu) * (lin + 1.0)
    y = jnp.dot(act.astype(jnp.bfloat16), wd_ref[0], preferred_element_type=jnp.float32) + bd_ref[0]
    y = y * g_ref[...]

    @pl.when(i >= 2)
    def _():
        wait_rows(scatter_copy, slot)

    ybuf[slot] = y

    def sbody(r, c):
        scatter_copy(sf_ref[0, 0, r], slot, r).start()
        return c
    lax.fori_loop(0, MOE_BLOCK, sbody, 0)

    @pl.when(i == nb - 1)
    def _():
        wait_rows(scatter_copy, slot)

        @pl.when(nb >= 2)
        def _():
            wait_rows(scatter_copy, nslot)


def _moe(h2d, block_e, slot_f, slot_gate, wgu_bf, bgu, wd_bf, bd):
    n_tok, d = h2d.shape
    n_rows = n_tok * TOP_K
    n_blocks = slot_f.shape[0]
    kern = functools.partial(_moe_kernel, n_rows=n_rows)
    grid_spec = pltpu.PrefetchScalarGridSpec(
        num_scalar_prefetch=1,
        grid=(n_blocks,),
        in_specs=[pl.BlockSpec((1, 1, MOE_BLOCK), lambda i, be: (i, 0, 0), memory_space=pltpu.SMEM),
                  pl.BlockSpec((1, 1, MOE_BLOCK), lambda i, be: (jnp.minimum(i + 1, n_blocks - 1), 0, 0),
                               memory_space=pltpu.SMEM),
                  pl.BlockSpec(memory_space=pl.ANY),
                  pl.BlockSpec((MOE_BLOCK, 1), lambda i, be: (i, 0)),
                  pl.BlockSpec((1, d, 2 * D_FF), lambda i, be: (be[i], 0, 0)),
                  pl.BlockSpec((1, 1, 2 * D_FF), lambda i, be: (be[i], 0, 0)),
                  pl.BlockSpec((1, D_FF, d), lambda i, be: (be[i], 0, 0)),
                  pl.BlockSpec((1, 1, d), lambda i, be: (be[i], 0, 0))],
        out_specs=pl.BlockSpec(memory_space=pl.ANY),
        scratch_shapes=[pltpu.VMEM((2, MOE_BLOCK, d), jnp.float32),
                        pltpu.VMEM((2, MOE_BLOCK, d), jnp.float32),
                        pltpu.SemaphoreType.DMA((2,)),
                        pltpu.SemaphoreType.DMA((2,))],
    )
    return pl.pallas_call(
        kern,
        grid_spec=grid_spec,
        out_shape=jax.ShapeDtypeStruct((n_rows + 2 * MOE_BLOCK, d), jnp.float32),
        compiler_params=_cparams(("arbitrary",)),
        name="moe_experts",
    )(block_e, slot_f, slot_f, h2d, slot_gate, wgu_bf, bgu, wd_bf, bd)


def _combine_kernel(x_ref, y_ref, mod_ref, g_ref, o_ref, *, final):
    d = x_ref.shape[1]
    y = y_ref[:, 0:d]
    for k in range(1, TOP_K):
        y = y + y_ref[:, k * d:(k + 1) * d]
    x2 = x_ref[...] + mod_ref[0, 5:6, :] * y
    if final:
        ms = jnp.mean(x2 * x2, axis=-1, keepdims=True)
        x2 = x2 * lax.rsqrt(ms + NORM_EPS) * g_ref[...]
    o_ref[...] = x2


def _combine(x1, y4, mod, g, seq, final):
    n, d = x1.shape
    tm = min(256, seq)
    per_b = seq // tm
    y2d = y4.reshape(y4.shape[0] // TOP_K, TOP_K * d)
    return pl.pallas_call(
        functools.partial(_combine_kernel, final=final),
        grid=(n // tm,),
        in_specs=[pl.BlockSpec((tm, d), lambda i: (i, 0)),
                  pl.BlockSpec((tm, TOP_K * d), lambda i: (i, 0)),
                  pl.BlockSpec((1, N_MOD, d), lambda i: (i // per_b, 0, 0)),
                  pl.BlockSpec((1, d), lambda i: (0, 0))],
        out_specs=pl.BlockSpec((tm, d), lambda i: (i, 0)),
        out_shape=jax.ShapeDtypeStruct((n, d), jnp.float32),
        compiler_params=_cparams(("arbitrary",)),
        name="combine",
    )(x1, y2d, mod, g)


def kernel(x, c, w_in, w_out, norm_mix, norm_ffn, w_ada, b_ada, lam_qk, diff_subln, sb_norm, rel_bias,
           w_router, b_router, w_gate_up, b_gate_up, w_down, b_down, final_norm):
    b, s, d = x.shape
    depth = w_in.shape[0]
    assert d == D_MODEL and s % ATT_TQ == 0 and ATT_TQ == ATT_TK
    n_tok = b * s

    mod_all = _ada_mod(c, w_ada, b_ada).reshape(depth, b, N_MOD, d)

    q_scale = HEAD_DIM ** -0.5
    colscale = jnp.concatenate([
        jnp.full((DIFF_WIDTH,), q_scale, jnp.float32), jnp.ones((2 * DIFF_WIDTH,), jnp.float32),
        jnp.full((SB_WIDTH,), -q_scale, jnp.float32), jnp.ones((2 * SB_WIDTH,), jnp.float32)]).reshape(1, PROJ_WIDTH)
    bias_tiles = _diff_bias_tiles(rel_bias)
    kk = jnp.arange(ATT_TK)
    tri = (kk[:, None] >= kk[None, :]).astype(jnp.bfloat16)

    xf = x.reshape(n_tok, d)
    for l in range(depth):
        mod = mod_all[l]
        lam_init = 0.8 - 0.6 * math.exp(-0.3 * l)
        lq = lam_qk[l].astype(jnp.float32)
        lam = jnp.exp(jnp.sum(lq[0] * lq[1])) - jnp.exp(jnp.sum(lq[2] * lq[3])) + lam_init
        lamv = jnp.full((1, LANES), lam, jnp.float32)

        proj = _inproj(xf.reshape(b, s, d), mod, norm_mix[l].reshape(1, d), w_in[l].astype(jnp.bfloat16), colscale)
        o_diff = _diff_attention(proj, bias_tiles, lamv, diff_subln[l].reshape(1, LANES), 1.0 - lam_init)
        o_sb = _sb_attention(proj, tri, jnp.tile(sb_norm[l], 2).reshape(1, LANES))

        x1, hffn, top_idx, gates = _outproj_router(
            xf, o_diff.reshape(n_tok, DIFF_WIDTH), o_sb.reshape(n_tok, SB_WIDTH), mod,
            norm_ffn[l].reshape(1, d), w_out[l].astype(jnp.bfloat16),
            w_router[l].T, b_router[l].reshape(N_EXPERTS, 1), s)

        block_e, slot_f, slot_gate = _routing_tables(top_idx, gates)
        y4 = _moe(hffn, block_e, slot_f, slot_gate,
                  w_gate_up[l].astype(jnp.bfloat16), b_gate_up[l].reshape(N_EXPERTS, 1, 2 * D_FF),
                  w_down[l].astype(jnp.bfloat16), b_down[l].reshape(N_EXPERTS, 1, d))
        xf = _combine(x1, y4, mod, final_norm.reshape(1, d), s, final=(l == depth - 1))
    return xf.reshape(b, s, d)
```

```python
import functools
import math

import jax
import jax.numpy as jnp
from jax import lax
from jax.experimental import pallas as pl
from jax.experimental.pallas import tpu as pltpu

D_MODEL = 1024
HEAD_DIM = 64
CHUNK = 64
DIFF_WIDTH = 512
N_DIFF_HEADS = 4
SB_WIDTH = 512
N_SB_HEADS = 8
PROJ_WIDTH = 3 * DIFF_WIDTH + 3 * SB_WIDTH
N_BUCKETS = 32
MAX_DISTANCE = 128
N_EXPERTS = 32
TOP_K = 4
D_FF = D_MODEL
SWIGLU_LIMIT = 7.0
SWIGLU_ALPHA = 1.702
MOE_BLOCK = 256
NORM_EPS = 1e-5
N_MOD = 6

LANES = 128
SUBLANES = 8
TOKEN_TILES = D_MODEL // LANES
N_COL_BLOCKS = PROJ_WIDTH // LANES
ATT_TQ = 512
ATT_TK = 512
SB_SUB = 256
SB_NSUB = 2
GATHER_AHEAD = 2
ROW_TILE = 512
COMBINE_TILE = 256
MASK_VALUE = -1e30
LOG2E = math.log2(math.e)
VMEM_LIMIT = 56 * 1024 * 1024

_NT_DIMS = (((1,), (1,)), ((), ()))

assert TOKEN_TILES == SUBLANES and ATT_TQ == ATT_TK and ATT_TQ % (SB_SUB * SB_NSUB) == 0


def _cparams(sem):
    return pltpu.CompilerParams(dimension_semantics=sem, vmem_limit_bytes=VMEM_LIMIT)


def _ada_kernel(c_ref, w_ref, b_ref, o_ref):
    c = c_ref[...]
    cond = c * jax.nn.sigmoid(c)
    o_ref[0] = jnp.dot(cond, w_ref[0], preferred_element_type=jnp.float32,
                       precision=lax.Precision.HIGHEST) + b_ref[0]


def _ada_mod(c, w_ada, b_ada):
    depth, d, n = w_ada.shape
    b = c.shape[0]
    tn = 1536
    return pl.pallas_call(
        _ada_kernel,
        grid=(depth, n // tn),
        in_specs=[pl.BlockSpec((b, d), lambda l, j: (0, 0)),
                  pl.BlockSpec((1, d, tn), lambda l, j: (l, 0, j)),
                  pl.BlockSpec((1, 1, tn), lambda l, j: (l, 0, j))],
        out_specs=pl.BlockSpec((1, b, tn), lambda l, j: (l, 0, j)),
        out_shape=jax.ShapeDtypeStruct((depth, b, n), jnp.float32),
        compiler_params=_cparams(("arbitrary", "arbitrary")),
        name="ada_mod",
    )(c, w_ada, b_ada.reshape(depth, 1, n))


def _inproj_kernel(x_ref, mod_ref, g_ref, w_ref, cs_ref, o_ref):
    x = x_ref[0]
    ms = jnp.mean(x * x, axis=-1, keepdims=True)
    y = x * lax.rsqrt(ms + NORM_EPS) * g_ref[...]
    h = y * (1.0 + mod_ref[0, 1:2, :]) + mod_ref[0, 0:1, :]
    hb = h.astype(jnp.bfloat16)
    nchunk = 512
    for c in range(PROJ_WIDTH // nchunk):
        acc = jnp.dot(hb, w_ref[:, c * nchunk:(c + 1) * nchunk], preferred_element_type=jnp.float32)
        acc = acc * cs_ref[:, c * nchunk:(c + 1) * nchunk]
        for j in range(nchunk // LANES):
            o_ref[0, c * (nchunk // LANES) + j] = acc[:, j * LANES:(j + 1) * LANES].astype(jnp.bfloat16)


def _inproj(x, mod, g, w_bf, colscale):
    b, s, d = x.shape
    tm = min(ROW_TILE, s)
    return pl.pallas_call(
        _inproj_kernel,
        grid=(b, s // tm),
        in_specs=[pl.BlockSpec((1, tm, d), lambda bi, i: (bi, i, 0)),
                  pl.BlockSpec((1, N_MOD, d), lambda bi, i: (bi, 0, 0)),
                  pl.BlockSpec((1, d), lambda bi, i: (0, 0)),
                  pl.BlockSpec((d, PROJ_WIDTH), lambda bi, i: (0, 0)),
                  pl.BlockSpec((1, PROJ_WIDTH), lambda bi, i: (0, 0))],
        out_specs=pl.BlockSpec((1, N_COL_BLOCKS, tm, LANES), lambda bi, i: (bi, 0, i, 0)),
        out_shape=jax.ShapeDtypeStruct((b, N_COL_BLOCKS, s, LANES), jnp.bfloat16),
        compiler_params=_cparams(("arbitrary", "arbitrary")),
        name="inproj",
    )(x, mod, g, w_bf, colscale)


def _split_halves(q):
    lane = lax.broadcasted_iota(jnp.int32, q.shape, 1)
    zero = jnp.zeros_like(q)
    return jnp.concatenate([jnp.where(lane < HEAD_DIM, q, zero), jnp.where(lane >= HEAD_DIM, q, zero)], axis=0)


def _lane_tile(x, n):
    return x if n == 1 else jnp.concatenate([x] * n, axis=1)


def _diff_kernel(q_ref, k_ref, v_ref, bias_ref, lam_ref, g_ref, o_ref, m_ref, l_ref, acc_ref, *, out_scale):
    i = pl.program_id(2)
    tq, tk = ATT_TQ, ATT_TK
    q2 = _split_halves(q_ref[0, 0])

    m_ref[...] = jnp.full(m_ref.shape, -jnp.inf, jnp.float32)
    l_ref[...] = jnp.zeros(l_ref.shape, jnp.float32)
    acc_ref[...] = jnp.zeros(acc_ref.shape, jnp.float32)

    def step(j, bias):
        off = pl.multiple_of(j * tk, tk)
        k = k_ref[0, 0, pl.ds(off, tk), :]
        v = v_ref[0, 0, pl.ds(off, tk), :]
        s = lax.dot_general(q2, k, _NT_DIMS, preferred_element_type=jnp.float32)
        if bias is not None:
            s = s + jnp.concatenate([bias, bias], axis=0)
        m_prev = m_ref[...]
        m_next = jnp.maximum(m_prev, jnp.max(s, axis=1, keepdims=True))
        p = jnp.exp2(s - _lane_tile(m_next, tk // LANES))
        alpha = jnp.exp2(m_prev - m_next)
        l_ref[...] = alpha * l_ref[...] + jnp.sum(p, axis=1, keepdims=True)
        acc_ref[...] = alpha * acc_ref[...] + jnp.dot(p.astype(jnp.bfloat16), v,
                                                      preferred_element_type=jnp.float32)
        m_ref[...] = m_next

    step(i, bias_ref[0, :, tk:2 * tk])

    @pl.when(i >= 1)
    def _():
        step(i - 1, bias_ref[0, :, 0:tk])

    def far(j, carry):
        step(j, None)
        return carry

    lax.fori_loop(0, jnp.maximum(i - 1, 0), far, 0)

    o = acc_ref[...] * (1.0 / l_ref[...])
    od = o[:tq] - lam_ref[...] * o[tq:]
    ms = jnp.mean(od * od, axis=-1, keepdims=True)
    y = od * lax.rsqrt(ms + NORM_EPS) * g_ref[...] * out_scale
    o_ref[0] = y.astype(jnp.bfloat16)


def _diff_attention(proj, bias, lamv, subln, out_scale):
    b, _, s, _ = proj.shape
    tq = ATT_TQ
    kern = functools.partial(_diff_kernel, out_scale=out_scale)
    return pl.pallas_call(
        kern,
        grid=(b, N_DIFF_HEADS, s // tq),
        in_specs=[pl.BlockSpec((1, 1, tq, LANES), lambda bi, h, i: (bi, h, i, 0)),
                  pl.BlockSpec((1, 1, s, LANES), lambda bi, h, i: (bi, N_DIFF_HEADS + h, 0, 0)),
                  pl.BlockSpec((1, 1, s, LANES), lambda bi, h, i: (bi, 2 * N_DIFF_HEADS + h, 0, 0)),
                  pl.BlockSpec((1, tq, 2 * ATT_TK), lambda bi, h, i: (h, 0, 0)),
                  pl.BlockSpec((1, LANES), lambda bi, h, i: (0, 0)),
                  pl.BlockSpec((1, LANES), lambda bi, h, i: (0, 0))],
        out_specs=pl.BlockSpec((1, tq, LANES), lambda bi, h, i: (bi, i, h)),
        out_shape=jax.ShapeDtypeStruct((b, s, DIFF_WIDTH), jnp.bfloat16),
        scratch_shapes=[pltpu.VMEM((2 * tq, LANES), jnp.float32),
                        pltpu.VMEM((2 * tq, LANES), jnp.float32),
                        pltpu.VMEM((2 * tq, LANES), jnp.float32)],
        compiler_params=_cparams(("arbitrary", "arbitrary", "arbitrary")),
        name="diff_attn",
    )(proj, proj, proj, bias, lamv, subln)


def _t5_bucket(rel):
    half = N_BUCKETS // 2
    max_exact = half // 2
    n = jnp.abs(rel)
    nf = jnp.maximum(n, 1).astype(jnp.float32)
    large = max_exact + (jnp.log(nf / max_exact) / math.log(MAX_DISTANCE / max_exact)
                         * (half - max_exact)).astype(jnp.int32)
    large = jnp.minimum(large, half - 1)
    return jnp.where(rel > 0, half, 0) + jnp.where(n < max_exact, n, large)


def _diff_bias_tiles(rel_bias):
    tq, tk = ATT_TQ, ATT_TK
    qpos = jnp.arange(tq)[:, None]
    kpos = jnp.arange(-tk, tk)[None, :]
    bucket = _t5_bucket(kpos - qpos)
    far_bucket = _t5_bucket(jnp.full((1, 1), -(tk + 1), jnp.int32))[0, 0]
    rb = rel_bias.astype(jnp.float32)
    bias = (rb[bucket] - rb[far_bucket]) * LOG2E
    allowed = (kpos // CHUNK) <= (qpos // CHUNK)
    bias = jnp.where(allowed[:, :, None], bias, MASK_VALUE)
    return jnp.transpose(bias, (2, 0, 1))


def _sb_kernel(q_ref, k_ref, v_ref, tri_ref, g_ref, o_ref, carry_ref, acc_ref):
    i = pl.program_id(2)
    tq, sub = ATT_TQ, SB_SUB
    q2 = _split_halves(q_ref[0, 0])
    tri = tri_ref[...]
    sign = jnp.uint32(0x80000000)

    carry_ref[...] = jnp.zeros(carry_ref.shape, jnp.float32)
    acc_ref[...] = jnp.zeros(acc_ref.shape, jnp.float32)

    def group(jb, masked):
        carry = carry_ref[...]
        a_parts, v_parts = [], []
        for t in range(SB_NSUB):
            off = pl.multiple_of((jb - t) * sub, sub)
            k = k_ref[0, 0, pl.ds(off, sub), :]
            v_parts.append(v_ref[0, 0, pl.ds(off, sub), :])
            u = lax.dot_general(q2, k, _NT_DIMS, preferred_element_type=jnp.float32)
            nabs = lax.bitcast_convert_type(lax.bitcast_convert_type(u, jnp.uint32) | sign, jnp.float32)
            lf = jnp.minimum(u, 0.0) - jnp.log2(1.0 + jnp.exp2(nabs))
            if masked:
                row = lax.broadcasted_iota(jnp.int32, (tq, sub), 0) + i * tq
                col = lax.broadcasted_iota(jnp.int32, (tq, sub), 1) + (jb - t) * sub
                strict = jnp.concatenate([col < row, col < row], axis=0)
                lf = jnp.where(strict, lf, 0.0)
            cum = jnp.dot(lf.astype(jnp.bfloat16), tri, preferred_element_type=jnp.float32)
            a = jnp.exp2(cum - u + _lane_tile(carry, sub // LANES))
            if masked:
                a = jnp.where(strict, a, 0.0)
            a_parts.append(a.astype(jnp.bfloat16))
            carry = carry + jnp.sum(lf, axis=1, keepdims=True)
        acc_ref[...] += jnp.dot(jnp.concatenate(a_parts, axis=1), jnp.concatenate(v_parts, axis=0),
                                preferred_element_type=jnp.float32)
        carry_ref[...] = carry

    nd = tq // sub
    for g in range(nd // SB_NSUB):
        group((i + 1) * nd - 1 - g * SB_NSUB, True)

    def far(t, c):
        group(i * nd - 1 - t * SB_NSUB, False)
        return c

    lax.fori_loop(0, i * (nd // SB_NSUB), far, 0)

    acc = acc_ref[...]
    lane = lax.broadcasted_iota(jnp.int32, (tq, LANES), 1)
    first = lane < HEAD_DIM
    o = jnp.where(first, acc[:tq], acc[tq:])
    sq = o * o
    ss_a = jnp.sum(jnp.where(first, sq, 0.0), axis=-1, keepdims=True)
    ss_b = jnp.sum(jnp.where(first, 0.0, sq), axis=-1, keepdims=True)
    ms = jnp.where(first, ss_a, ss_b) * (1.0 / HEAD_DIM)
    o_ref[0] = (o * lax.rsqrt(ms + NORM_EPS) * g_ref[...]).astype(jnp.bfloat16)


def _sb_attention(proj, tri, gain2):
    b, _, s, _ = proj.shape
    tq = ATT_TQ
    npair = N_SB_HEADS // 2
    base = 3 * N_DIFF_HEADS
    return pl.pallas_call(
        _sb_kernel,
        grid=(b, npair, s // tq),
        in_specs=[pl.BlockSpec((1, 1, tq, LANES), lambda bi, h, i: (bi, base + h, i, 0)),
                  pl.BlockSpec((1, 1, s, LANES), lambda bi, h, i: (bi, base + npair + h, 0, 0)),
                  pl.BlockSpec((1, 1, s, LANES), lambda bi, h, i: (bi, base + 2 * npair + h, 0, 0)),
                  pl.BlockSpec((SB_SUB, SB_SUB), lambda bi, h, i: (0, 0)),
                  pl.BlockSpec((1, LANES), lambda bi, h, i: (0, 0))],
        out_specs=pl.BlockSpec((1, tq, LANES), lambda bi, h, i: (bi, i, h)),
        out_shape=jax.ShapeDtypeStruct((b, s, SB_WIDTH), jnp.bfloat16),
        scratch_shapes=[pltpu.VMEM((2 * tq, LANES), jnp.float32),
                        pltpu.VMEM((2 * tq, LANES), jnp.float32)],
        compiler_params=_cparams(("arbitrary", "arbitrary", "arbitrary")),
        name="sb_attn",
    )(proj, proj, proj, tri, gain2)


def _outproj_kernel(x_ref, od_ref, os_ref, mod_ref, g_ref, wo_ref, wr_ref, br_ref,
                    x1_ref, h_ref, idx_ref, gate_ref):
    tm = x_ref.shape[0]
    y = jnp.dot(od_ref[...], wo_ref[0:DIFF_WIDTH, :], preferred_element_type=jnp.float32)
    y = y + jnp.dot(os_ref[...], wo_ref[DIFF_WIDTH:, :], preferred_element_type=jnp.float32)
    x1 = x_ref[...] + mod_ref[0, 2:3, :] * y
    x1_ref[...] = x1
    ms = jnp.mean(x1 * x1, axis=-1, keepdims=True)
    h = x1 * lax.rsqrt(ms + NORM_EPS) * g_ref[...]
    h = h * (1.0 + mod_ref[0, 4:5, :]) + mod_ref[0, 3:4, :]
    for c in range(TOKEN_TILES):
        h_ref[pl.ds(c, tm, stride=SUBLANES), :] = h[:, c * LANES:(c + 1) * LANES]
    logits = jnp.dot(h, wr_ref[...], preferred_element_type=jnp.float32,
                     precision=lax.Precision.HIGHEST) + br_ref[...]
    eidx = lax.broadcasted_iota(jnp.int32, logits.shape, 1)
    vals, idxs = [], []
    for _ in range(TOP_K):
        mx = jnp.max(logits, axis=1, keepdims=True)
        sel = jnp.min(jnp.where(logits == mx, eidx, N_EXPERTS), axis=1, keepdims=True)
        vals.append(mx)
        idxs.append(sel)
        logits = jnp.where(eidx == sel, -jnp.inf, logits)
    ex = [jnp.exp(v - vals[0]) for v in vals]
    inv = 1.0 / (ex[0] + ex[1] + ex[2] + ex[3])
    lane = lax.broadcasted_iota(jnp.int32, (tm, LANES), 1)
    gates = jnp.zeros((tm, LANES), jnp.float32)
    picks = jnp.zeros((tm, LANES), jnp.int32)
    for k in range(TOP_K):
        gates = jnp.where(lane == k, ex[k] * inv, gates)
        picks = jnp.where(lane == k, idxs[k], picks)
    gate_ref[...] = gates
    idx_ref[...] = picks


def _outproj_router(x2d, od, osb, mod, g, wo_bf, wr, br, seq):
    n, d = x2d.shape
    tm = min(ROW_TILE, seq)
    per_b = seq // tm
    return pl.pallas_call(
        _outproj_kernel,
        grid=(n // tm,),
        in_specs=[pl.BlockSpec((tm, d), lambda i: (i, 0)),
                  pl.BlockSpec((tm, DIFF_WIDTH), lambda i: (i, 0)),
                  pl.BlockSpec((tm, SB_WIDTH), lambda i: (i, 0)),
                  pl.BlockSpec((1, N_MOD, d), lambda i: (i // per_b, 0, 0)),
                  pl.BlockSpec((1, d), lambda i: (0, 0)),
                  pl.BlockSpec((d, d), lambda i: (0, 0)),
                  pl.BlockSpec((d, N_EXPERTS), lambda i: (0, 0)),
                  pl.BlockSpec((1, N_EXPERTS), lambda i: (0, 0))],
        out_specs=[pl.BlockSpec((tm, d), lambda i: (i, 0)),
                   pl.BlockSpec((tm * SUBLANES, LANES), lambda i: (i, 0)),
                   pl.BlockSpec((tm, LANES), lambda i: (i, 0)),
                   pl.BlockSpec((tm, LANES), lambda i: (i, 0))],
        out_shape=[jax.ShapeDtypeStruct((n, d), jnp.float32),
                   jax.ShapeDtypeStruct((n * SUBLANES, LANES), jnp.float32),
                   jax.ShapeDtypeStruct((n, LANES), jnp.int32),
                   jax.ShapeDtypeStruct((n, LANES), jnp.float32)],
        compiler_params=_cparams(("arbitrary",)),
        name="outproj_router",
    )(x2d, od, osb, mod, g, wo_bf, wr, br)


def _moe_n_blocks(n_tok):
    n_rows = n_tok * TOP_K
    return -(-(n_rows + N_EXPERTS * (MOE_BLOCK - 1)) // MOE_BLOCK)


def _routing_tables(picks):
    n_tok = picks.shape[0]
    n_rows = n_tok * TOP_K
    n_blocks = _moe_n_blocks(n_tok)
    flat_e = picks.reshape(-1)
    keys = flat_e * n_rows + jnp.arange(n_rows, dtype=jnp.int32)
    skeys = lax.sort(keys)
    counts = jnp.sum(flat_e[None, :] == jnp.arange(N_EXPERTS, dtype=jnp.int32)[:, None], axis=1).astype(jnp.int32)
    starts = jnp.cumsum(counts) - counts
    padded = (counts + MOE_BLOCK - 1) // MOE_BLOCK * MOE_BLOCK
    pend = jnp.cumsum(padded)
    pstart = pend - padded
    blk0 = jnp.arange(n_blocks, dtype=jnp.int32) * MOE_BLOCK
    block_e = jnp.minimum(jnp.sum(pend[None, :] <= blk0[:, None], axis=1), N_EXPERTS - 1).astype(jnp.int32)
    shift = (starts - pstart)[block_e]
    limit = (pstart + counts)[block_e]
    slot = blk0[:, None] + jnp.arange(MOE_BLOCK, dtype=jnp.int32)[None, :]
    valid = slot < limit[:, None]
    last = (starts + counts - 1)[block_e]
    src = jnp.clip(jnp.minimum(slot + shift[:, None], last[:, None]), 0, n_rows - 1)
    f = jnp.take(skeys, src.reshape(-1), indices_are_sorted=True).reshape(n_blocks, MOE_BLOCK) % n_rows
    spare = n_rows + slot % (2 * MOE_BLOCK)
    slot_dst = jnp.where(valid, f, spare).astype(jnp.int32)
    slot_tok = jnp.where(valid, f // TOP_K, 0).astype(jnp.int32)
    return block_e, slot_tok.reshape(n_blocks, 1, MOE_BLOCK), slot_dst.reshape(n_blocks, 1, MOE_BLOCK)


def _moe_kernel(be_ref, tok0_ref, tok1_ref, tokn_ref, dst_ref, h_hbm, wgu_ref, bgu_ref, wd_ref, bd_ref, y_hbm,
                xbuf, ybuf, gsem, ssem):
    i = pl.program_id(0)
    nb = pl.num_programs(0)
    rows = MOE_BLOCK
    nx = GATHER_AHEAD + 1
    xslot = i % nx
    yslot = i % 2

    def start_gather(idx_ref, slot):
        for r in range(rows):
            pltpu.make_async_copy(h_hbm.at[pl.ds(idx_ref[0, 0, r] * SUBLANES, SUBLANES)],
                                  xbuf.at[slot, pl.ds(r * SUBLANES, SUBLANES)], gsem.at[slot]).start()

    def wait_gather(slot):
        pltpu.make_async_copy(h_hbm.at[pl.ds(0, rows * SUBLANES)], xbuf.at[slot], gsem.at[slot]).wait()

    def wait_scatter(slot):
        pltpu.make_async_copy(ybuf.at[slot], y_hbm.at[pl.ds(0, rows * SUBLANES)], ssem.at[slot]).wait()

    @pl.when(i == 0)
    def _():
        start_gather(tok0_ref, 0)
        start_gather(tok1_ref, 1)

    wait_gather(xslot)
    xb = jnp.concatenate([xbuf[xslot, pl.ds(c, rows, stride=SUBLANES), :] for c in range(TOKEN_TILES)],
                         axis=1).astype(jnp.bfloat16)
    gu = jnp.dot(xb, wgu_ref[0], preferred_element_type=jnp.float32) + bgu_ref[0]
    glu = jnp.minimum(gu[:, :D_FF], SWIGLU_LIMIT)
    lin = jnp.clip(gu[:, D_FF:], -SWIGLU_LIMIT, SWIGLU_LIMIT)
    act = glu * jax.nn.sigmoid(SWIGLU_ALPHA * glu) * (lin + 1.0)
    y = jnp.dot(act.astype(jnp.bfloat16), wd_ref[0], preferred_element_type=jnp.float32) + bd_ref[0]

    start_gather(tokn_ref, (i + GATHER_AHEAD) % nx)

    @pl.when(i >= 2)
    def _():
        wait_scatter(yslot)

    for c in range(TOKEN_TILES):
        ybuf[yslot, pl.ds(c, rows, stride=SUBLANES), :] = y[:, c * LANES:(c + 1) * LANES]
    for r in range(rows):
        pltpu.make_async_copy(ybuf.at[yslot, pl.ds(r * SUBLANES, SUBLANES)],
                              y_hbm.at[pl.ds(dst_ref[0, 0, r] * SUBLANES, SUBLANES)], ssem.at[yslot]).start()

    @pl.when(i == nb - 1)
    def _():
        wait_gather((i + 1) % nx)
        wait_gather((i + 2) % nx)
        wait_scatter(yslot)

        @pl.when(nb >= 2)
        def _():
            wait_scatter(1 - yslot)


def _moe(h_tiles, block_e, slot_tok, slot_dst, wgu_bf, bgu, wd_bf, bd):
    n_tok = h_tiles.shape[0] // SUBLANES
    d = D_MODEL
    n_rows = n_tok * TOP_K
    n_blocks = slot_tok.shape[0]
    rows = MOE_BLOCK
    assert n_blocks >= GATHER_AHEAD

    def idx_block(ahead):
        return pl.BlockSpec((1, 1, rows), lambda i, be: (jnp.minimum(i + ahead, n_blocks - 1), 0, 0),
                            memory_space=pltpu.SMEM)

    grid_spec = pltpu.PrefetchScalarGridSpec(
        num_scalar_prefetch=1,
        grid=(n_blocks,),
        in_specs=[idx_block(0), idx_block(1), idx_block(GATHER_AHEAD), idx_block(0),
                  pl.BlockSpec(memory_space=pl.ANY),
                  pl.BlockSpec((1, d, 2 * D_FF), lambda i, be: (be[i], 0, 0)),
                  pl.BlockSpec((1, 1, 2 * D_FF), lambda i, be: (be[i], 0, 0)),
                  pl.BlockSpec((1, D_FF, d), lambda i, be: (be[i], 0, 0)),
                  pl.BlockSpec((1, 1, d), lambda i, be: (be[i], 0, 0))],
        out_specs=pl.BlockSpec(memory_space=pl.ANY),
        scratch_shapes=[pltpu.VMEM((GATHER_AHEAD + 1, rows * SUBLANES, LANES), jnp.float32),
                        pltpu.VMEM((2, rows * SUBLANES, LANES), jnp.float32),
                        pltpu.SemaphoreType.DMA((GATHER_AHEAD + 1,)),
                        pltpu.SemaphoreType.DMA((2,))],
    )
    return pl.pallas_call(
        _moe_kernel,
        grid_spec=grid_spec,
        out_shape=jax.ShapeDtypeStruct(((n_rows + 2 * rows) * SUBLANES, LANES), jnp.float32),
        compiler_params=_cparams(("arbitrary",)),
        name="moe_experts",
    )(block_e, slot_tok, slot_tok, slot_tok, slot_dst, h_tiles, wgu_bf, bgu, wd_bf, bd)


def _combine_kernel(x_ref, y_ref, gate_ref, mod_ref, g_ref, o_ref, *, final):
    tm = x_ref.shape[0]
    per_tok = TOP_K * SUBLANES
    gk = [gate_ref[:, k:k + 1] for k in range(TOP_K)]
    parts = []
    for c in range(TOKEN_TILES):
        acc = gk[0] * y_ref[pl.ds(c, tm, stride=per_tok), :]
        for k in range(1, TOP_K):
            acc = acc + gk[k] * y_ref[pl.ds(k * SUBLANES + c, tm, stride=per_tok), :]
        parts.append(acc)
    x2 = x_ref[...] + mod_ref[0, 5:6, :] * jnp.concatenate(parts, axis=1)
    if final:
        ms = jnp.mean(x2 * x2, axis=-1, keepdims=True)
        x2 = x2 * lax.rsqrt(ms + NORM_EPS) * g_ref[...]
    o_ref[...] = x2


def _combine(x1, y_tiles, gates, mod, g, seq, final):
    n, d = x1.shape
    tm = min(COMBINE_TILE, seq)
    per_b = seq // tm
    per_tok = TOP_K * SUBLANES
    return pl.pallas_call(
        functools.partial(_combine_kernel, final=final),
        grid=(n // tm,),
        in_specs=[pl.BlockSpec((tm, d), lambda i: (i, 0)),
                  pl.BlockSpec((tm * per_tok, LANES), lambda i: (i, 0)),
                  pl.BlockSpec((tm, LANES), lambda i: (i, 0)),
                  pl.BlockSpec((1, N_MOD, d), lambda i: (i // per_b, 0, 0)),
                  pl.BlockSpec((1, d), lambda i: (0, 0))],
        out_specs=pl.BlockSpec((tm, d), lambda i: (i, 0)),
        out_shape=jax.ShapeDtypeStruct((n, d), jnp.float32),
        compiler_params=_cparams(("arbitrary",)),
        name="combine",
    )(x1, y_tiles, gates, mod, g)


def kernel(x, c, w_in, w_out, norm_mix, norm_ffn, w_ada, b_ada, lam_qk, diff_subln, sb_norm, rel_bias,
           w_router, b_router, w_gate_up, b_gate_up, w_down, b_down, final_norm):
    b, s, d = x.shape
    depth = w_in.shape[0]
    assert d == D_MODEL and s % ATT_TQ == 0
    n_tok = b * s

    mod_all = _ada_mod(c, w_ada, b_ada).reshape(depth, b, N_MOD, d)

    q_scale = HEAD_DIM ** -0.5 * LOG2E
    colscale = jnp.concatenate([
        jnp.full((DIFF_WIDTH,), q_scale, jnp.float32), jnp.ones((2 * DIFF_WIDTH,), jnp.float32),
        jnp.full((SB_WIDTH,), -q_scale, jnp.float32), jnp.ones((2 * SB_WIDTH,), jnp.float32)]).reshape(1, PROJ_WIDTH)
    bias_tiles = _diff_bias_tiles(rel_bias)
    kk = jnp.arange(SB_SUB)
    tri = (kk[:, None] >= kk[None, :]).astype(jnp.bfloat16)

    xf = x.reshape(n_tok, d)
    for l in range(depth):
        mod = mod_all[l]
        lam_init = 0.8 - 0.6 * math.exp(-0.3 * l)
        lq = lam_qk[l].astype(jnp.float32)
        lam = jnp.exp(jnp.sum(lq[0] * lq[1])) - jnp.exp(jnp.sum(lq[2] * lq[3])) + lam_init
        lamv = jnp.full((1, LANES), lam, jnp.float32)

        proj = _inproj(xf.reshape(b, s, d), mod, norm_mix[l].reshape(1, d), w_in[l].astype(jnp.bfloat16), colscale)
        o_diff = _diff_attention(proj, bias_tiles, lamv, diff_subln[l].reshape(1, LANES), 1.0 - lam_init)
        o_sb = _sb_attention(proj, tri, jnp.tile(sb_norm[l], 2).reshape(1, LANES))

        x1, h_tiles, picks, gates = _outproj_router(
            xf, o_diff.reshape(n_tok, DIFF_WIDTH), o_sb.reshape(n_tok, SB_WIDTH), mod,
            norm_ffn[l].reshape(1, d), w_out[l].astype(jnp.bfloat16),
            w_router[l], b_router[l].reshape(1, N_EXPERTS), s)

        block_e, slot_tok, slot_dst = _routing_tables(picks[:, :TOP_K])
        y_tiles = _moe(h_tiles, block_e, slot_tok, slot_dst,
                       w_gate_up[l].astype(jnp.bfloat16), b_gate_up[l].reshape(N_EXPERTS, 1, 2 * D_FF),
                       w_down[l].astype(jnp.bfloat16), b_down[l].reshape(N_EXPERTS, 1, d))
        xf = _combine(x1, y_tiles, gates, mod, final_norm.reshape(1, d), s, final=(l == depth - 1))
    return xf.reshape(b, s, d)
```

```python
import functools
import math

import jax
import jax.numpy as jnp
from jax import lax
from jax.experimental import pallas as pl
from jax.experimental.pallas import tpu as pltpu

D_MODEL = 1024
HEAD_DIM = 64
CHUNK = 64
DIFF_WIDTH = 512
N_DIFF_HEADS = 4
SB_WIDTH = 512
N_SB_HEADS = 8
PROJ_WIDTH = 3 * DIFF_WIDTH + 3 * SB_WIDTH
N_BUCKETS = 32
MAX_DISTANCE = 128
N_EXPERTS = 32
TOP_K = 4
D_FF = D_MODEL
SWIGLU_LIMIT = 7.0
SWIGLU_ALPHA = 1.702
MOE_BLOCK = 256
NORM_EPS = 1e-5
N_MOD = 6

LANES = 128
SUBLANES = 8
TOKEN_TILES = D_MODEL // LANES
N_COL_BLOCKS = PROJ_WIDTH // LANES
ATT_TQ = 512
ATT_TK = 512
SB_SUB = 256
SB_NSUB = 2
SB_DEAD_LOG2 = -256.0
GATHER_AHEAD = 2
ROW_TILE = 512
COMBINE_TILE = 256
MASK_VALUE = -1e30
LOG2E = math.log2(math.e)
VMEM_LIMIT = 56 * 1024 * 1024

_NT_DIMS = (((1,), (1,)), ((), ()))

assert TOKEN_TILES == SUBLANES and ATT_TQ == ATT_TK and ATT_TQ % (SB_SUB * SB_NSUB) == 0


def _cparams(sem):
    return pltpu.CompilerParams(dimension_semantics=sem, vmem_limit_bytes=VMEM_LIMIT)


def _ada_kernel(c_ref, w_ref, b_ref, o_ref):
    c = c_ref[...]
    cond = c * jax.nn.sigmoid(c)
    o_ref[0] = jnp.dot(cond, w_ref[0], preferred_element_type=jnp.float32,
                       precision=lax.Precision.HIGHEST) + b_ref[0]


def _ada_mod(c, w_ada, b_ada):
    depth, d, n = w_ada.shape
    b = c.shape[0]
    tn = 1536
    return pl.pallas_call(
        _ada_kernel,
        grid=(depth, n // tn),
        in_specs=[pl.BlockSpec((b, d), lambda l, j: (0, 0)),
                  pl.BlockSpec((1, d, tn), lambda l, j: (l, 0, j)),
                  pl.BlockSpec((1, 1, tn), lambda l, j: (l, 0, j))],
        out_specs=pl.BlockSpec((1, b, tn), lambda l, j: (l, 0, j)),
        out_shape=jax.ShapeDtypeStruct((depth, b, n), jnp.float32),
        compiler_params=_cparams(("arbitrary", "arbitrary")),
        name="ada_mod",
    )(c, w_ada, b_ada.reshape(depth, 1, n))


def _inproj_kernel(x_ref, mod_ref, g_ref, w_ref, cs_ref, o_ref):
    x = x_ref[0]
    ms = jnp.mean(x * x, axis=-1, keepdims=True)
    y = x * lax.rsqrt(ms + NORM_EPS) * g_ref[...]
    h = y * (1.0 + mod_ref[0, 1:2, :]) + mod_ref[0, 0:1, :]
    hb = h.astype(jnp.bfloat16)
    nchunk = 512
    for c in range(PROJ_WIDTH // nchunk):
        acc = jnp.dot(hb, w_ref[:, c * nchunk:(c + 1) * nchunk], preferred_element_type=jnp.float32)
        acc = acc * cs_ref[:, c * nchunk:(c + 1) * nchunk]
        for j in range(nchunk // LANES):
            o_ref[0, c * (nchunk // LANES) + j] = acc[:, j * LANES:(j + 1) * LANES].astype(jnp.bfloat16)


def _inproj(x, mod, g, w_bf, colscale):
    b, s, d = x.shape
    tm = min(ROW_TILE, s)
    return pl.pallas_call(
        _inproj_kernel,
        grid=(b, s // tm),
        in_specs=[pl.BlockSpec((1, tm, d), lambda bi, i: (bi, i, 0)),
                  pl.BlockSpec((1, N_MOD, d), lambda bi, i: (bi, 0, 0)),
                  pl.BlockSpec((1, d), lambda bi, i: (0, 0)),
                  pl.BlockSpec((d, PROJ_WIDTH), lambda bi, i: (0, 0)),
                  pl.BlockSpec((1, PROJ_WIDTH), lambda bi, i: (0, 0))],
        out_specs=pl.BlockSpec((1, N_COL_BLOCKS, tm, LANES), lambda bi, i: (bi, 0, i, 0)),
        out_shape=jax.ShapeDtypeStruct((b, N_COL_BLOCKS, s, LANES), jnp.bfloat16),
        compiler_params=_cparams(("arbitrary", "arbitrary")),
        name="inproj",
    )(x, mod, g, w_bf, colscale)


def _split_halves(q):
    lane = lax.broadcasted_iota(jnp.int32, q.shape, 1)
    zero = jnp.zeros_like(q)
    return jnp.concatenate([jnp.where(lane < HEAD_DIM, q, zero), jnp.where(lane >= HEAD_DIM, q, zero)], axis=0)


def _lane_tile(x, n):
    return x if n == 1 else jnp.concatenate([x] * n, axis=1)


def _diff_kernel(q_ref, k_ref, v_ref, bias_ref, lam_ref, g_ref, o_ref, m_ref, l_ref, acc_ref, *, out_scale):
    i = pl.program_id(2)
    tq, tk = ATT_TQ, ATT_TK
    q2 = _split_halves(q_ref[0, 0])

    m_ref[...] = jnp.full(m_ref.shape, -jnp.inf, jnp.float32)
    l_ref[...] = jnp.zeros(l_ref.shape, jnp.float32)
    acc_ref[...] = jnp.zeros(acc_ref.shape, jnp.float32)

    def step(j, bias):
        off = pl.multiple_of(j * tk, tk)
        k = k_ref[0, 0, pl.ds(off, tk), :]
        v = v_ref[0, 0, pl.ds(off, tk), :]
        s = lax.dot_general(q2, k, _NT_DIMS, preferred_element_type=jnp.float32)
        if bias is not None:
            s = s + jnp.concatenate([bias, bias], axis=0)
        m_prev = m_ref[...]
        m_next = jnp.maximum(m_prev, jnp.max(s, axis=1, keepdims=True))
        p = jnp.exp2(s - _lane_tile(m_next, tk // LANES))
        alpha = jnp.exp2(m_prev - m_next)
        l_ref[...] = alpha * l_ref[...] + jnp.sum(p, axis=1, keepdims=True)
        acc_ref[...] = alpha * acc_ref[...] + jnp.dot(p.astype(jnp.bfloat16), v,
                                                      preferred_element_type=jnp.float32)
        m_ref[...] = m_next

    step(i, bias_ref[0, :, tk:2 * tk])

    @pl.when(i >= 1)
    def _():
        step(i - 1, bias_ref[0, :, 0:tk])

    def far(j, carry):
        step(j, None)
        return carry

    lax.fori_loop(0, jnp.maximum(i - 1, 0), far, 0)

    o = acc_ref[...] * (1.0 / l_ref[...])
    od = o[:tq] - lam_ref[...] * o[tq:]
    ms = jnp.mean(od * od, axis=-1, keepdims=True)
    y = od * lax.rsqrt(ms + NORM_EPS) * g_ref[...] * out_scale
    o_ref[0] = y.astype(jnp.bfloat16)


def _diff_attention(proj, bias, lamv, subln, out_scale):
    b, _, s, _ = proj.shape
    tq = ATT_TQ
    kern = functools.partial(_diff_kernel, out_scale=out_scale)
    return pl.pallas_call(
        kern,
        grid=(b, N_DIFF_HEADS, s // tq),
        in_specs=[pl.BlockSpec((1, 1, tq, LANES), lambda bi, h, i: (bi, h, i, 0)),
                  pl.BlockSpec((1, 1, s, LANES), lambda bi, h, i: (bi, N_DIFF_HEADS + h, 0, 0)),
                  pl.BlockSpec((1, 1, s, LANES), lambda bi, h, i: (bi, 2 * N_DIFF_HEADS + h, 0, 0)),
                  pl.BlockSpec((1, tq, 2 * ATT_TK), lambda bi, h, i: (h, 0, 0)),
                  pl.BlockSpec((1, LANES), lambda bi, h, i: (0, 0)),
                  pl.BlockSpec((1, LANES), lambda bi, h, i: (0, 0))],
        out_specs=pl.BlockSpec((1, tq, LANES), lambda bi, h, i: (bi, i, h)),
        out_shape=jax.ShapeDtypeStruct((b, s, DIFF_WIDTH), jnp.bfloat16),
        scratch_shapes=[pltpu.VMEM((2 * tq, LANES), jnp.float32),
                        pltpu.VMEM((2 * tq, LANES), jnp.float32),
                        pltpu.VMEM((2 * tq, LANES), jnp.float32)],
        compiler_params=_cparams(("arbitrary", "arbitrary", "arbitrary")),
        name="diff_attn",
    )(proj, proj, proj, bias, lamv, subln)


def _t5_bucket(rel):
    half = N_BUCKETS // 2
    max_exact = half // 2
    n = jnp.abs(rel)
    nf = jnp.maximum(n, 1).astype(jnp.float32)
    large = max_exact + (jnp.log(nf / max_exact) / math.log(MAX_DISTANCE / max_exact)
                         * (half - max_exact)).astype(jnp.int32)
    large = jnp.minimum(large, half - 1)
    return jnp.where(rel > 0, half, 0) + jnp.where(n < max_exact, n, large)


def _diff_bias_tiles(rel_bias):
    tq, tk = ATT_TQ, ATT_TK
    qpos = jnp.arange(tq)[:, None]
    kpos = jnp.arange(-tk, tk)[None, :]
    bucket = _t5_bucket(kpos - qpos)
    far_bucket = _t5_bucket(jnp.full((1, 1), -(tk + 1), jnp.int32))[0, 0]
    rb = rel_bias.astype(jnp.float32)
    bias = (rb[bucket] - rb[far_bucket]) * LOG2E
    allowed = (kpos // CHUNK) <= (qpos // CHUNK)
    bias = jnp.where(allowed[:, :, None], bias, MASK_VALUE)
    return jnp.transpose(bias, (2, 0, 1))


def _sb_kernel(q_ref, k_ref, v_ref, tri_ref, g_ref, o_ref, carry_ref, acc_ref):
    i = pl.program_id(2)
    tq, sub = ATT_TQ, SB_SUB
    q2 = _split_halves(q_ref[0, 0])
    tri = tri_ref[...]
    sign = jnp.uint32(0x80000000)

    carry_ref[...] = jnp.zeros(carry_ref.shape, jnp.float32)
    acc_ref[...] = jnp.zeros(acc_ref.shape, jnp.float32)

    def group(jb, masked):
        carry = carry_ref[...]
        a_parts, v_parts = [], []
        for t in range(SB_NSUB):
            off = pl.multiple_of((jb - t) * sub, sub)
            k = k_ref[0, 0, pl.ds(off, sub), :]
            v_parts.append(v_ref[0, 0, pl.ds(off, sub), :])
            u = lax.dot_general(q2, k, _NT_DIMS, preferred_element_type=jnp.float32)
            nabs = lax.bitcast_convert_type(lax.bitcast_convert_type(u, jnp.uint32) | sign, jnp.float32)
            lf = jnp.minimum(u, 0.0) - jnp.log2(1.0 + jnp.exp2(nabs))
            if masked:
                row = lax.broadcasted_iota(jnp.int32, (tq, sub), 0) + i * tq
                col = lax.broadcasted_iota(jnp.int32, (tq, sub), 1) + (jb - t) * sub
                strict = jnp.concatenate([col < row, col < row], axis=0)
                lf = jnp.where(strict, lf, 0.0)
            cum = jnp.dot(lf.astype(jnp.bfloat16), tri, preferred_element_type=jnp.float32)
            a = jnp.exp2(cum - u + _lane_tile(carry, sub // LANES))
            if masked:
                a = jnp.where(strict, a, 0.0)
            a_parts.append(a.astype(jnp.bfloat16))
            carry = carry + jnp.sum(lf, axis=1, keepdims=True)
        acc_ref[...] += jnp.dot(jnp.concatenate(a_parts, axis=1), jnp.concatenate(v_parts, axis=0),
                                preferred_element_type=jnp.float32)
        carry_ref[...] = carry

    nd = tq // sub
    for g in range(nd // SB_NSUB):
        group((i + 1) * nd - 1 - g * SB_NSUB, True)

    n_far = i * (nd // SB_NSUB)

    def alive(state):
        t, top = state
        return jnp.logical_and(t < n_far, top > SB_DEAD_LOG2)

    def far(state):
        t, _ = state
        group(i * nd - 1 - t * SB_NSUB, False)
        return t + 1, jnp.max(carry_ref[...])

    lax.while_loop(alive, far, (jnp.int32(0), jnp.max(carry_ref[...])))

    acc = acc_ref[...]
    lane = lax.broadcasted_iota(jnp.int32, (tq, LANES), 1)
    first = lane < HEAD_DIM
    o = jnp.where(first, acc[:tq], acc[tq:])
    sq = o * o
    ss_a = jnp.sum(jnp.where(first, sq, 0.0), axis=-1, keepdims=True)
    ss_b = jnp.sum(jnp.where(first, 0.0, sq), axis=-1, keepdims=True)
    ms = jnp.where(first, ss_a, ss_b) * (1.0 / HEAD_DIM)
    o_ref[0] = (o * lax.rsqrt(ms + NORM_EPS) * g_ref[...]).astype(jnp.bfloat16)


def _sb_attention(proj, tri, gain2):
    b, _, s, _ = proj.shape
    tq = ATT_TQ
    npair = N_SB_HEADS // 2
    base = 3 * N_DIFF_HEADS
    return pl.pallas_call(
        _sb_kernel,
        grid=(b, npair, s // tq),
        in_specs=[pl.BlockSpec((1, 1, tq, LANES), lambda bi, h, i: (bi, base + h, i, 0)),
                  pl.BlockSpec((1, 1, s, LANES), lambda bi, h, i: (bi, base + npair + h, 0, 0)),
                  pl.BlockSpec((1, 1, s, LANES), lambda bi, h, i: (bi, base + 2 * npair + h, 0, 0)),
                  pl.BlockSpec((SB_SUB, SB_SUB), lambda bi, h, i: (0, 0)),
                  pl.BlockSpec((1, LANES), lambda bi, h, i: (0, 0))],
        out_specs=pl.BlockSpec((1, tq, LANES), lambda bi, h, i: (bi, i, h)),
        out_shape=jax.ShapeDtypeStruct((b, s, SB_WIDTH), jnp.bfloat16),
        scratch_shapes=[pltpu.VMEM((2 * tq, LANES), jnp.float32),
                        pltpu.VMEM((2 * tq, LANES), jnp.float32)],
        compiler_params=_cparams(("arbitrary", "arbitrary", "arbitrary")),
        name="sb_attn",
    )(proj, proj, proj, tri, gain2)


def _outproj_kernel(x_ref, od_ref, os_ref, mod_ref, g_ref, wo_ref, wr_ref, br_ref,
                    x1_ref, h_ref, route_ref):
    tm = x_ref.shape[0]
    y = jnp.dot(od_ref[...], wo_ref[0:DIFF_WIDTH, :], preferred_element_type=jnp.float32)
    y = y + jnp.dot(os_ref[...], wo_ref[DIFF_WIDTH:, :], preferred_element_type=jnp.float32)
    x1 = x_ref[...] + mod_ref[0, 2:3, :] * y
    x1_ref[...] = x1
    ms = jnp.mean(x1 * x1, axis=-1, keepdims=True)
    h = x1 * lax.rsqrt(ms + NORM_EPS) * g_ref[...]
    h = h * (1.0 + mod_ref[0, 4:5, :]) + mod_ref[0, 3:4, :]
    for c in range(TOKEN_TILES):
        h_ref[pl.ds(c, tm, stride=SUBLANES), :] = h[:, c * LANES:(c + 1) * LANES]
    logits = lax.dot_general(wr_ref[...], h, _NT_DIMS, preferred_element_type=jnp.float32,
                             precision=lax.Precision.HIGHEST) + br_ref[...]
    eidx = lax.broadcasted_iota(jnp.int32, logits.shape, 0)
    vals, idxs = [], []
    for _ in range(TOP_K):
        mx = jnp.max(logits, axis=0, keepdims=True)
        sel = jnp.min(jnp.where(logits == mx, eidx, N_EXPERTS), axis=0, keepdims=True)
        vals.append(mx)
        idxs.append(sel)
        logits = jnp.where(eidx == sel, -jnp.inf, logits)
    ex = [jnp.exp(v - vals[0]) for v in vals]
    inv = 1.0 / (ex[0] + ex[1] + ex[2] + ex[3])
    rows = [e * inv for e in ex] + [lax.bitcast_convert_type(s, jnp.float32) for s in idxs]
    packed = jnp.concatenate(rows + [jnp.zeros((LANES - 2 * TOP_K, tm), jnp.float32)], axis=0)
    route_ref[...] = packed.T


def _outproj_router(x2d, od, osb, mod, g, wo_bf, wr, br, seq):
    n, d = x2d.shape
    tm = min(ROW_TILE, seq)
    per_b = seq // tm
    return pl.pallas_call(
        _outproj_kernel,
        grid=(n // tm,),
        in_specs=[pl.BlockSpec((tm, d), lambda i: (i, 0)),
                  pl.BlockSpec((tm, DIFF_WIDTH), lambda i: (i, 0)),
                  pl.BlockSpec((tm, SB_WIDTH), lambda i: (i, 0)),
                  pl.BlockSpec((1, N_MOD, d), lambda i: (i // per_b, 0, 0)),
                  pl.BlockSpec((1, d), lambda i: (0, 0)),
                  pl.BlockSpec((d, d), lambda i: (0, 0)),
                  pl.BlockSpec((N_EXPERTS, d), lambda i: (0, 0)),
                  pl.BlockSpec((N_EXPERTS, 1), lambda i: (0, 0))],
        out_specs=[pl.BlockSpec((tm, d), lambda i: (i, 0)),
                   pl.BlockSpec((tm * SUBLANES, LANES), lambda i: (i, 0)),
                   pl.BlockSpec((tm, LANES), lambda i: (i, 0))],
        out_shape=[jax.ShapeDtypeStruct((n, d), jnp.float32),
                   jax.ShapeDtypeStruct((n * SUBLANES, LANES), jnp.float32),
                   jax.ShapeDtypeStruct((n, LANES), jnp.float32)],
        compiler_params=_cparams(("arbitrary",)),
        name="outproj_router",
    )(x2d, od, osb, mod, g, wo_bf, wr, br)


def _moe_n_blocks(n_tok):
    n_rows = n_tok * TOP_K
    return -(-(n_rows + N_EXPERTS * (MOE_BLOCK - 1)) // MOE_BLOCK)


N_SPARE = 2 * MOE_BLOCK


def _routing_tables(picks):
    n_tok = picks.shape[0]
    n_rows = n_tok * TOP_K
    n_blocks = _moe_n_blocks(n_tok)
    n_slots = n_blocks * MOE_BLOCK
    n_fill = MOE_BLOCK - 1
    stride = n_rows + MOE_BLOCK
    flat_e = picks.reshape(-1)
    experts = jnp.arange(N_EXPERTS, dtype=jnp.int32)
    counts = jnp.sum(flat_e[None, :] == experts[:, None], axis=1).astype(jnp.int32)
    need = (-counts) % MOE_BLOCK
    real = flat_e * stride + jnp.arange(n_rows, dtype=jnp.int32)
    j = jnp.arange(n_fill, dtype=jnp.int32)
    fill = jnp.where(j[None, :] < need[:, None], experts[:, None] * stride + n_rows + j[None, :],
                     N_EXPERTS * stride).reshape(-1)
    tail = jnp.full((n_slots - n_rows - N_EXPERTS * n_fill,), N_EXPERTS * stride, jnp.int32)
    skeys = lax.sort(jnp.concatenate([real, fill, tail])).reshape(n_blocks, MOE_BLOCK)
    block_e = jnp.minimum(skeys[:, 0] // stride, N_EXPERTS - 1).astype(jnp.int32)
    f = skeys % stride
    valid = jnp.logical_and(f < n_rows, skeys < N_EXPERTS * stride)
    slot = jnp.arange(n_slots, dtype=jnp.int32).reshape(n_blocks, MOE_BLOCK)
    slot_dst = jnp.where(valid, (f % TOP_K) * n_tok + f // TOP_K, n_rows + slot % N_SPARE).astype(jnp.int32)
    slot_tok = jnp.where(valid, f // TOP_K, 0).astype(jnp.int32)
    return block_e, slot_tok.reshape(n_blocks, 1, MOE_BLOCK), slot_dst.reshape(n_blocks, 1, MOE_BLOCK)


def _moe_kernel(be_ref, tok0_ref, tok1_ref, tokn_ref, dst_ref, h_hbm, wgu_ref, bgu_ref, wd_ref, bd_ref, y_hbm,
                xbuf, ybuf, gsem, ssem, *, n_rows):
    i = pl.program_id(0)
    nb = pl.num_programs(0)
    rows = MOE_BLOCK
    nx = GATHER_AHEAD + 1
    xslot = i % nx
    yslot = i % 2

    def start_gather(idx_ref, slot):
        for r in range(rows):
            pltpu.make_async_copy(h_hbm.at[pl.ds(idx_ref[0, 0, r] * SUBLANES, SUBLANES)],
                                  xbuf.at[slot, pl.ds(r * SUBLANES, SUBLANES)], gsem.at[slot]).start()

    def wait_gather(slot):
        pltpu.make_async_copy(h_hbm.at[pl.ds(0, rows * SUBLANES)], xbuf.at[slot], gsem.at[slot]).wait()

    def wait_scatter(slot):
        pltpu.make_async_copy(ybuf.at[slot], y_hbm.at[pl.ds(0, rows * SUBLANES)], ssem.at[slot]).wait()

    @pl.when(i == 0)
    def _():
        start_gather(tok0_ref, 0)
        start_gather(tok1_ref, 1)
        ybuf[1] = jnp.zeros(ybuf.shape[1:], jnp.float32)
        for part in range(N_SPARE // rows):
            pltpu.make_async_copy(ybuf.at[1], y_hbm.at[pl.ds((n_rows + part * rows) * SUBLANES, rows * SUBLANES)],
                                  ssem.at[1]).start()
        for part in range(N_SPARE // rows):
            wait_scatter(1)

    wait_gather(xslot)
    xb = jnp.concatenate([xbuf[xslot, pl.ds(c, rows, stride=SUBLANES), :] for c in range(TOKEN_TILES)],
                         axis=1).astype(jnp.bfloat16)
    gu = jnp.dot(xb, wgu_ref[0], preferred_element_type=jnp.float32) + bgu_ref[0]
    glu = jnp.minimum(gu[:, :D_FF], SWIGLU_LIMIT)
    lin = jnp.clip(gu[:, D_FF:], -SWIGLU_LIMIT, SWIGLU_LIMIT)
    act = glu * jax.nn.sigmoid(SWIGLU_ALPHA * glu) * (lin + 1.0)
    y = jnp.dot(act.astype(jnp.bfloat16), wd_ref[0], preferred_element_type=jnp.float32) + bd_ref[0]

    start_gather(tokn_ref, (i + GATHER_AHEAD) % nx)

    @pl.when(i >= 2)
    def _():
        wait_scatter(yslot)

    for c in range(TOKEN_TILES):
        ybuf[yslot, pl.ds(c, rows, stride=SUBLANES), :] = y[:, c * LANES:(c + 1) * LANES]
    for r in range(rows):
        pltpu.make_async_copy(ybuf.at[yslot, pl.ds(r * SUBLANES, SUBLANES)],
                              y_hbm.at[pl.ds(dst_ref[0, 0, r] * SUBLANES, SUBLANES)], ssem.at[yslot]).start()

    @pl.when(i == nb - 1)
    def _():
        wait_gather((i + 1) % nx)
        wait_gather((i + 2) % nx)
        wait_scatter(yslot)

        @pl.when(nb >= 2)
        def _():
            wait_scatter(1 - yslot)


def _moe(h_tiles, block_e, slot_tok, slot_dst, wgu_bf, bgu, wd_bf, bd):
    n_tok = h_tiles.shape[0] // SUBLANES
    d = D_MODEL
    n_rows = n_tok * TOP_K
    n_blocks = slot_tok.shape[0]
    rows = MOE_BLOCK
    assert n_blocks >= GATHER_AHEAD

    def idx_block(ahead):
        return pl.BlockSpec((1, 1, rows), lambda i, be: (jnp.minimum(i + ahead, n_blocks - 1), 0, 0),
                            memory_space=pltpu.SMEM)

    grid_spec = pltpu.PrefetchScalarGridSpec(
        num_scalar_prefetch=1,
        grid=(n_blocks,),
        in_specs=[idx_block(0), idx_block(1), idx_block(GATHER_AHEAD), idx_block(0),
                  pl.BlockSpec(memory_space=pl.ANY),
                  pl.BlockSpec((1, d, 2 * D_FF), lambda i, be: (be[i], 0, 0)),
                  pl.BlockSpec((1, 1, 2 * D_FF), lambda i, be: (be[i], 0, 0)),
                  pl.BlockSpec((1, D_FF, d), lambda i, be: (be[i], 0, 0)),
                  pl.BlockSpec((1, 1, d), lambda i, be: (be[i], 0, 0))],
        out_specs=pl.BlockSpec(memory_space=pl.ANY),
        scratch_shapes=[pltpu.VMEM((GATHER_AHEAD + 1, rows * SUBLANES, LANES), jnp.float32),
                        pltpu.VMEM((2, rows * SUBLANES, LANES), jnp.float32),
                        pltpu.SemaphoreType.DMA((GATHER_AHEAD + 1,)),
                        pltpu.SemaphoreType.DMA((2,))],
    )
    return pl.pallas_call(
        functools.partial(_moe_kernel, n_rows=n_rows),
        grid_spec=grid_spec,
        out_shape=jax.ShapeDtypeStruct(((n_rows + N_SPARE) * SUBLANES, LANES), jnp.float32),
        compiler_params=_cparams(("arbitrary",)),
        name="moe_experts",
    )(block_e, slot_tok, slot_tok, slot_tok, slot_dst, h_tiles, wgu_bf, bgu, wd_bf, bd)


def _combine_kernel(x_ref, y0_ref, y1_ref, y2_ref, y3_ref, gate_ref, mod_ref, g_ref, o_ref, *, final):
    tm = x_ref.shape[0]
    y_refs = (y0_ref, y1_ref, y2_ref, y3_ref)
    gk = [jnp.broadcast_to(gate_ref[:, k:k + 1], (tm, LANES)) for k in range(TOP_K)]
    parts = []
    for c in range(TOKEN_TILES):
        acc = gk[0] * y_refs[0][pl.ds(c, tm, stride=SUBLANES), :]
        for k in range(1, TOP_K):
            acc = acc + gk[k] * y_refs[k][pl.ds(c, tm, stride=SUBLANES), :]
        parts.append(acc)
    x2 = x_ref[...] + mod_ref[0, 5:6, :] * jnp.concatenate(parts, axis=1)
    if final:
        ms = jnp.mean(x2 * x2, axis=-1, keepdims=True)
        x2 = x2 * lax.rsqrt(ms + NORM_EPS) * g_ref[...]
    o_ref[...] = x2


def _combine(x1, y_tiles, gates, mod, g, seq, final):
    n, d = x1.shape
    tm = min(COMBINE_TILE, seq)
    per_b = seq // tm
    plane_blocks = n // tm
    assert n % tm == 0 and TOP_K == 4

    def plane_spec(k):
        return pl.BlockSpec((tm * SUBLANES, LANES), lambda i: (k * plane_blocks + i, 0))

    return pl.pallas_call(
        functools.partial(_combine_kernel, final=final),
        grid=(n // tm,),
        in_specs=[pl.BlockSpec((tm, d), lambda i: (i, 0)),
                  plane_spec(0), plane_spec(1), plane_spec(2), plane_spec(3),
                  pl.BlockSpec((tm, LANES), lambda i: (i, 0)),
                  pl.BlockSpec((1, N_MOD, d), lambda i: (i // per_b, 0, 0)),
                  pl.BlockSpec((1, d), lambda i: (0, 0))],
        out_specs=pl.BlockSpec((tm, d), lambda i: (i, 0)),
        out_shape=jax.ShapeDtypeStruct((n, d), jnp.float32),
        compiler_params=_cparams(("arbitrary",)),
        name="combine",
    )(x1, y_tiles, y_tiles, y_tiles, y_tiles, gates, mod, g)


def kernel(x, c, w_in, w_out, norm_mix, norm_ffn, w_ada, b_ada, lam_qk, diff_subln, sb_norm, rel_bias,
           w_router, b_router, w_gate_up, b_gate_up, w_down, b_down, final_norm):
    b, s, d = x.shape
    depth = w_in.shape[0]
    assert d == D_MODEL and s % ATT_TQ == 0
    n_tok = b * s

    mod_all = _ada_mod(c, w_ada, b_ada).reshape(depth, b, N_MOD, d)

    q_scale = HEAD_DIM ** -0.5 * LOG2E
    colscale = jnp.concatenate([
        jnp.full((DIFF_WIDTH,), q_scale, jnp.float32), jnp.ones((2 * DIFF_WIDTH,), jnp.float32),
        jnp.full((SB_WIDTH,), -q_scale, jnp.float32), jnp.ones((2 * SB_WIDTH,), jnp.float32)]).reshape(1, PROJ_WIDTH)
    bias_tiles = _diff_bias_tiles(rel_bias)
    kk = jnp.arange(SB_SUB)
    tri = (kk[:, None] >= kk[None, :]).astype(jnp.bfloat16)

    xf = x.reshape(n_tok, d)
    for l in range(depth):
        mod = mod_all[l]
        lam_init = 0.8 - 0.6 * math.exp(-0.3 * l)
        lq = lam_qk[l].astype(jnp.float32)
        lam = jnp.exp(jnp.sum(lq[0] * lq[1])) - jnp.exp(jnp.sum(lq[2] * lq[3])) + lam_init
        lamv = jnp.full((1, LANES), lam, jnp.float32)

        proj = _inproj(xf.reshape(b, s, d), mod, norm_mix[l].reshape(1, d), w_in[l].astype(jnp.bfloat16), colscale)
        o_diff = _diff_attention(proj, bias_tiles, lamv, diff_subln[l].reshape(1, LANES), 1.0 - lam_init)
        o_sb = _sb_attention(proj, tri, jnp.tile(sb_norm[l], 2).reshape(1, LANES))

        x1, h_tiles, route = _outproj_router(
            xf, o_diff.reshape(n_tok, DIFF_WIDTH), o_sb.reshape(n_tok, SB_WIDTH), mod,
            norm_ffn[l].reshape(1, d), w_out[l].astype(jnp.bfloat16),
            w_router[l].T, b_router[l].reshape(N_EXPERTS, 1), s)
        picks = lax.bitcast_convert_type(route[:, TOP_K:2 * TOP_K], jnp.int32)

        block_e, slot_tok, slot_dst = _routing_tables(picks)
        y_tiles = _moe(h_tiles, block_e, slot_tok, slot_dst,
                       w_gate_up[l].astype(jnp.bfloat16), b_gate_up[l].reshape(N_EXPERTS, 1, 2 * D_FF),
                       w_down[l].astype(jnp.bfloat16), b_down[l].reshape(N_EXPERTS, 1, d))
        xf = _combine(x1, y_tiles, route, mod, final_norm.reshape(1, d), s, final=(l == depth - 1))
    return xf.reshape(b, s, d)
```

```python
import functools
import math

import jax
import jax.numpy as jnp
from jax import lax
from jax.experimental import pallas as pl
from jax.experimental.pallas import tpu as pltpu

D_MODEL = 1024
HEAD_DIM = 64
CHUNK = 64
DIFF_WIDTH = 512
N_DIFF_HEADS = 4
SB_WIDTH = 512
N_SB_HEADS = 8
PROJ_WIDTH = 3 * DIFF_WIDTH + 3 * SB_WIDTH
N_BUCKETS = 32
MAX_DISTANCE = 128
N_EXPERTS = 32
TOP_K = 4
D_FF = D_MODEL
SWIGLU_LIMIT = 7.0
SWIGLU_ALPHA = 1.702
MOE_BLOCK = 256
NORM_EPS = 1e-5
N_MOD = 6

LANES = 128
SUBLANES = 8
TOKEN_TILES = D_MODEL // LANES
N_COL_BLOCKS = PROJ_WIDTH // LANES
ATT_TQ = 512
ATT_TK = 512
SB_SUB = 256
SB_NSUB = 2
SB_DEAD_LOG2 = -256.0
GATHER_AHEAD = 2
ROW_TILE = 512
COMBINE_TILE = 256
MASK_VALUE = -1e30
LOG2E = math.log2(math.e)
VMEM_LIMIT = 56 * 1024 * 1024

_NT_DIMS = (((1,), (1,)), ((), ()))

assert TOKEN_TILES == SUBLANES and ATT_TQ == ATT_TK and ATT_TQ % (SB_SUB * SB_NSUB) == 0


def _cparams(sem):
    return pltpu.CompilerParams(dimension_semantics=sem, vmem_limit_bytes=VMEM_LIMIT)


def _ada_kernel(c_ref, w_ref, b_ref, o_ref):
    c = c_ref[...]
    cond = c * jax.nn.sigmoid(c)
    o_ref[0] = jnp.dot(cond, w_ref[0], preferred_element_type=jnp.float32,
                       precision=lax.Precision.HIGHEST) + b_ref[0]


def _ada_mod(c, w_ada, b_ada):
    depth, d, n = w_ada.shape
    b = c.shape[0]
    tn = 1536
    return pl.pallas_call(
        _ada_kernel,
        grid=(depth, n // tn),
        in_specs=[pl.BlockSpec((b, d), lambda l, j: (0, 0)),
                  pl.BlockSpec((1, d, tn), lambda l, j: (l, 0, j)),
                  pl.BlockSpec((1, 1, tn), lambda l, j: (l, 0, j))],
        out_specs=pl.BlockSpec((1, b, tn), lambda l, j: (l, 0, j)),
        out_shape=jax.ShapeDtypeStruct((depth, b, n), jnp.float32),
        compiler_params=_cparams(("arbitrary", "arbitrary")),
        name="ada_mod",
    )(c, w_ada, b_ada.reshape(depth, 1, n))


def _inproj_kernel(x_ref, mod_ref, g_ref, w_ref, cs_ref, o_ref):
    x = x_ref[0]
    ms = jnp.mean(x * x, axis=-1, keepdims=True)
    y = x * lax.rsqrt(ms + NORM_EPS) * g_ref[...]
    h = y * (1.0 + mod_ref[0, 1:2, :]) + mod_ref[0, 0:1, :]
    hb = h.astype(jnp.bfloat16)
    nchunk = 512
    for c in range(PROJ_WIDTH // nchunk):
        acc = jnp.dot(hb, w_ref[:, c * nchunk:(c + 1) * nchunk], preferred_element_type=jnp.float32)
        acc = acc * cs_ref[:, c * nchunk:(c + 1) * nchunk]
        for j in range(nchunk // LANES):
            o_ref[0, c * (nchunk // LANES) + j] = acc[:, j * LANES:(j + 1) * LANES].astype(jnp.bfloat16)


def _inproj(x, mod, g, w_bf, colscale):
    b, s, d = x.shape
    tm = min(ROW_TILE, s)
    return pl.pallas_call(
        _inproj_kernel,
        grid=(b, s // tm),
        in_specs=[pl.BlockSpec((1, tm, d), lambda bi, i: (bi, i, 0)),
                  pl.BlockSpec((1, N_MOD, d), lambda bi, i: (bi, 0, 0)),
                  pl.BlockSpec((1, d), lambda bi, i: (0, 0)),
                  pl.BlockSpec((d, PROJ_WIDTH), lambda bi, i: (0, 0)),
                  pl.BlockSpec((1, PROJ_WIDTH), lambda bi, i: (0, 0))],
        out_specs=pl.BlockSpec((1, N_COL_BLOCKS, tm, LANES), lambda bi, i: (bi, 0, i, 0)),
        out_shape=jax.ShapeDtypeStruct((b, N_COL_BLOCKS, s, LANES), jnp.bfloat16),
        compiler_params=_cparams(("arbitrary", "arbitrary")),
        name="inproj",
    )(x, mod, g, w_bf, colscale)


def _split_halves(q):
    lane = lax.broadcasted_iota(jnp.int32, q.shape, 1)
    zero = jnp.zeros_like(q)
    return jnp.concatenate([jnp.where(lane < HEAD_DIM, q, zero), jnp.where(lane >= HEAD_DIM, q, zero)], axis=0)


def _lane_tile(x, n):
    return x if n == 1 else jnp.concatenate([x] * n, axis=1)


DIFF_SHIFT_MARGIN = 1.01
DIFF_SHIFT_SPAN = 100.0


def _diff_kernel(q_ref, k_ref, v_ref, bias_ref, aux_ref, g_ref, o_ref, m_ref, l_ref, acc_ref, kn_ref, *, out_scale):
    i = pl.program_id(2)
    tq, tk = ATT_TQ, ATT_TK
    seq = k_ref.shape[2]
    q2 = _split_halves(q_ref[0, 0])
    first = lax.broadcasted_iota(jnp.int32, (tk, LANES), 1) < HEAD_DIM

    @pl.when(i == 0)
    def _():
        def chunk(c, mx):
            kc = k_ref[0, 0, pl.ds(pl.multiple_of(c * tk, tk), tk), :].astype(jnp.float32)
            sq = kc * kc
            lo = jnp.max(jnp.sum(jnp.where(first, sq, 0.0), axis=1, keepdims=True), axis=0, keepdims=True)
            hi = jnp.max(jnp.sum(jnp.where(first, 0.0, sq), axis=1, keepdims=True), axis=0, keepdims=True)
            return jnp.maximum(mx[0], lo), jnp.maximum(mx[1], hi)
        zero = jnp.zeros((1, 1), jnp.float32)
        lo, hi = lax.fori_loop(0, seq // tk, chunk, (zero, zero))
        kn_ref[0:1, :] = jnp.broadcast_to(jnp.sqrt(lo), (1, LANES))
        kn_ref[1:2, :] = jnp.broadcast_to(jnp.sqrt(hi), (1, LANES))

    qf = q2.astype(jnp.float32)
    qn = jnp.sqrt(jnp.sum(qf * qf, axis=1, keepdims=True))
    bound = jnp.concatenate([qn[:tq] * kn_ref[0:1, :], qn[tq:] * kn_ref[1:2, :]], axis=0) * DIFF_SHIFT_MARGIN
    fixed_ok = jnp.max(2.0 * bound + aux_ref[2:3, :]) <= DIFF_SHIFT_SPAN

    l_ref[...] = jnp.zeros(l_ref.shape, jnp.float32)
    acc_ref[...] = jnp.zeros(acc_ref.shape, jnp.float32)

    def scores(j, bias):
        k = k_ref[0, 0, pl.ds(pl.multiple_of(j * tk, tk), tk), :]
        s = lax.dot_general(q2, k, _NT_DIMS, preferred_element_type=jnp.float32)
        return s if bias is None else s + jnp.concatenate([bias, bias], axis=0)

    def sweep(update, unroll):
        update(i, bias_ref[0, :, tk:2 * tk])

        @pl.when(i >= 1)
        def _():
            update(i - 1, bias_ref[0, :, 0:tk])

        n_far = jnp.maximum(i - 1, 0)

        def far(t, carry):
            for u in range(unroll):
                update(unroll * t + u, None)
            return carry
        lax.fori_loop(0, n_far // unroll, far, 0)

        def rest(j, carry):
            update(j, None)
            return carry
        lax.fori_loop(n_far - n_far % unroll, n_far, rest, 0)

    @pl.when(fixed_ok)
    def _():
        m_ref[...] = bound + aux_ref[1:2, :]

        def update(j, bias):
            v = v_ref[0, 0, pl.ds(pl.multiple_of(j * tk, tk), tk), :]
            p = jnp.exp2(scores(j, bias) - _lane_tile(m_ref[...], tk // LANES))
            part = p[:, 0:LANES]
            for c in range(1, tk // LANES):
                part = part + p[:, c * LANES:(c + 1) * LANES]
            l_ref[...] += part
            acc_ref[...] += jnp.dot(p.astype(jnp.bfloat16), v, preferred_element_type=jnp.float32)

        sweep(update, 2)
        l_ref[...] = jnp.broadcast_to(jnp.sum(l_ref[...], axis=1, keepdims=True), l_ref.shape)

    @pl.when(jnp.logical_not(fixed_ok))
    def _():
        m_ref[...] = jnp.full(m_ref.shape, -jnp.inf, jnp.float32)

        def update(j, bias):
            v = v_ref[0, 0, pl.ds(pl.multiple_of(j * tk, tk), tk), :]
            s = scores(j, bias)
            m_prev = m_ref[...]
            m_next = jnp.maximum(m_prev, jnp.max(s, axis=1, keepdims=True))
            p = jnp.exp2(s - _lane_tile(m_next, tk // LANES))
            alpha = jnp.exp2(m_prev - m_next)
            l_ref[...] = alpha * l_ref[...] + jnp.sum(p, axis=1, keepdims=True)
            acc_ref[...] = alpha * acc_ref[...] + jnp.dot(p.astype(jnp.bfloat16), v,
                                                          preferred_element_type=jnp.float32)
            m_ref[...] = m_next

        sweep(update, 1)

    o = acc_ref[...] * (1.0 / l_ref[...])
    od = o[:tq] - aux_ref[0:1, :] * o[tq:]
    ms = jnp.mean(od * od, axis=-1, keepdims=True)
    y = od * lax.rsqrt(ms + NORM_EPS) * g_ref[...] * out_scale
    o_ref[0] = y.astype(jnp.bfloat16)


def _diff_attention(proj, bias, aux, subln, out_scale):
    b, _, s, _ = proj.shape
    tq = ATT_TQ
    kern = functools.partial(_diff_kernel, out_scale=out_scale)
    return pl.pallas_call(
        kern,
        grid=(b, N_DIFF_HEADS, s // tq),
        in_specs=[pl.BlockSpec((1, 1, tq, LANES), lambda bi, h, i: (bi, h, i, 0)),
                  pl.BlockSpec((1, 1, s, LANES), lambda bi, h, i: (bi, N_DIFF_HEADS + h, 0, 0)),
                  pl.BlockSpec((1, 1, s, LANES), lambda bi, h, i: (bi, 2 * N_DIFF_HEADS + h, 0, 0)),
                  pl.BlockSpec((1, tq, 2 * ATT_TK), lambda bi, h, i: (h, 0, 0)),
                  pl.BlockSpec((SUBLANES, LANES), lambda bi, h, i: (0, 0)),
                  pl.BlockSpec((1, LANES), lambda bi, h, i: (0, 0))],
        out_specs=pl.BlockSpec((1, tq, LANES), lambda bi, h, i: (bi, i, h)),
        out_shape=jax.ShapeDtypeStruct((b, s, DIFF_WIDTH), jnp.bfloat16),
        scratch_shapes=[pltpu.VMEM((2 * tq, LANES), jnp.float32),
                        pltpu.VMEM((2 * tq, LANES), jnp.float32),
                        pltpu.VMEM((2 * tq, LANES), jnp.float32),
                        pltpu.VMEM((SUBLANES, LANES), jnp.float32)],
        compiler_params=_cparams(("arbitrary", "arbitrary", "arbitrary")),
        name="diff_attn",
    )(proj, proj, proj, bias, aux, subln)


def _t5_bucket(rel):
    half = N_BUCKETS // 2
    max_exact = half // 2
    n = jnp.abs(rel)
    nf = jnp.maximum(n, 1).astype(jnp.float32)
    large = max_exact + (jnp.log(nf / max_exact) / math.log(MAX_DISTANCE / max_exact)
                         * (half - max_exact)).astype(jnp.int32)
    large = jnp.minimum(large, half - 1)
    return jnp.where(rel > 0, half, 0) + jnp.where(n < max_exact, n, large)


def _diff_bias_tiles(rel_bias):
    tq, tk = ATT_TQ, ATT_TK
    span = 2 * tk + tq
    rel = jnp.arange(span, dtype=jnp.int32) - (tk + tq - 1)
    far_bucket = _t5_bucket(jnp.full((1,), -(tk + 1), jnp.int32))[0]
    rb = rel_bias.astype(jnp.float32)
    by_rel = ((rb[_t5_bucket(rel)] - rb[far_bucket]) * LOG2E).T
    nh = by_rel.shape[0]
    shifted = jnp.tile(by_rel, (1, tq))[:, :tq * (span - 1)].reshape(nh, tq, span - 1)
    bias = shifted[:, :, tq - 1:tq - 1 + 2 * tk]
    qpos = jnp.arange(tq)[:, None]
    kpos = jnp.arange(-tk, tk)[None, :]
    allowed = (kpos // CHUNK) <= (qpos // CHUNK)
    top = jnp.maximum(jnp.max(by_rel), 0.0)
    spread = top - jnp.minimum(jnp.min(by_rel), 0.0)
    return jnp.where(allowed[None], bias, MASK_VALUE), top, spread


def _sb_kernel(q_ref, k_ref, v_ref, tri_ref, g_ref, o_ref, carry_ref, acc_ref):
    i = pl.program_id(2)
    tq, sub = ATT_TQ, SB_SUB
    q2 = _split_halves(q_ref[0, 0])
    tri = tri_ref[...]
    sign = jnp.uint32(0x80000000)

    carry_ref[...] = jnp.zeros(carry_ref.shape, jnp.float32)
    acc_ref[...] = jnp.zeros(acc_ref.shape, jnp.float32)

    def group(jb, masked):
        carry = carry_ref[...]
        a_parts, v_parts = [], []
        for t in range(SB_NSUB):
            off = pl.multiple_of((jb - t) * sub, sub)
            k = k_ref[0, 0, pl.ds(off, sub), :]
            v_parts.append(v_ref[0, 0, pl.ds(off, sub), :])
            u = lax.dot_general(q2, k, _NT_DIMS, preferred_element_type=jnp.float32)
            nabs = lax.bitcast_convert_type(lax.bitcast_convert_type(u, jnp.uint32) | sign, jnp.float32)
            lf = jnp.minimum(u, 0.0) - jnp.log2(1.0 + jnp.exp2(nabs))
            if masked:
                row = lax.broadcasted_iota(jnp.int32, (tq, sub), 0) + i * tq
                col = lax.broadcasted_iota(jnp.int32, (tq, sub), 1) + (jb - t) * sub
                strict = jnp.concatenate([col < row, col < row], axis=0)
                lf = jnp.where(strict, lf, 0.0)
            cum = jnp.dot(lf.astype(jnp.bfloat16), tri, preferred_element_type=jnp.float32)
            a = jnp.exp2(cum - u + _lane_tile(carry, sub // LANES))
            if masked:
                a = jnp.where(strict, a, 0.0)
            a_parts.append(a.astype(jnp.bfloat16))
            carry = carry + jnp.sum(lf, axis=1, keepdims=True)
        acc_ref[...] += jnp.dot(jnp.concatenate(a_parts, axis=1), jnp.concatenate(v_parts, axis=0),
                                preferred_element_type=jnp.float32)
        carry_ref[...] = carry

    nd = tq // sub
    for g in range(nd // SB_NSUB):
        group((i + 1) * nd - 1 - g * SB_NSUB, True)

    n_far = i * (nd // SB_NSUB)

    def alive(state):
        t, top = state
        return jnp.logical_and(t < n_far, top > SB_DEAD_LOG2)

    def far(state):
        t, _ = state
        group(i * nd - 1 - t * SB_NSUB, False)
        return t + 1, jnp.max(carry_ref[...])

    lax.while_loop(alive, far, (jnp.int32(0), jnp.max(carry_ref[...])))

    acc = acc_ref[...]
    lane = lax.broadcasted_iota(jnp.int32, (tq, LANES), 1)
    first = lane < HEAD_DIM
    o = jnp.where(first, acc[:tq], acc[tq:])
    sq = o * o
    ss_a = jnp.sum(jnp.where(first, sq, 0.0), axis=-1, keepdims=True)
    ss_b = jnp.sum(jnp.where(first, 0.0, sq), axis=-1, keepdims=True)
    ms = jnp.where(first, ss_a, ss_b) * (1.0 / HEAD_DIM)
    o_ref[0] = (o * lax.rsqrt(ms + NORM_EPS) * g_ref[...]).astype(jnp.bfloat16)


def _sb_attention(proj, tri, gain2):
    b, _, s, _ = proj.shape
    tq = ATT_TQ
    npair = N_SB_HEADS // 2
    base = 3 * N_DIFF_HEADS
    return pl.pallas_call(
        _sb_kernel,
        grid=(b, npair, s // tq),
        in_specs=[pl.BlockSpec((1, 1, tq, LANES), lambda bi, h, i: (bi, base + h, i, 0)),
                  pl.BlockSpec((1, 1, s, LANES), lambda bi, h, i: (bi, base + npair + h, 0, 0)),
                  pl.BlockSpec((1, 1, s, LANES), lambda bi, h, i: (bi, base + 2 * npair + h, 0, 0)),
                  pl.BlockSpec((SB_SUB, SB_SUB), lambda bi, h, i: (0, 0)),
                  pl.BlockSpec((1, LANES), lambda bi, h, i: (0, 0))],
        out_specs=pl.BlockSpec((1, tq, LANES), lambda bi, h, i: (bi, i, h)),
        out_shape=jax.ShapeDtypeStruct((b, s, SB_WIDTH), jnp.bfloat16),
        scratch_shapes=[pltpu.VMEM((2 * tq, LANES), jnp.float32),
                        pltpu.VMEM((2 * tq, LANES), jnp.float32)],
        compiler_params=_cparams(("arbitrary", "arbitrary", "arbitrary")),
        name="sb_attn",
    )(proj, proj, proj, tri, gain2)


def _outproj_kernel(x_ref, od_ref, os_ref, mod_ref, g_ref, wo_ref, wr_ref, br_ref,
                    x1_ref, h_ref, route_ref):
    tm = x_ref.shape[0]
    y = jnp.dot(od_ref[...], wo_ref[0:DIFF_WIDTH, :], preferred_element_type=jnp.float32)
    y = y + jnp.dot(os_ref[...], wo_ref[DIFF_WIDTH:, :], preferred_element_type=jnp.float32)
    x1 = x_ref[...] + mod_ref[0, 2:3, :] * y
    x1_ref[...] = x1
    ms = jnp.mean(x1 * x1, axis=-1, keepdims=True)
    h = x1 * lax.rsqrt(ms + NORM_EPS) * g_ref[...]
    h = h * (1.0 + mod_ref[0, 4:5, :]) + mod_ref[0, 3:4, :]
    for c in range(TOKEN_TILES):
        h_ref[pl.ds(c, tm, stride=SUBLANES), :] = h[:, c * LANES:(c + 1) * LANES]
    logits = lax.dot_general(wr_ref[...], h, _NT_DIMS, preferred_element_type=jnp.float32,
                             precision=lax.Precision.HIGHEST) + br_ref[...]
    eidx = lax.broadcasted_iota(jnp.int32, logits.shape, 0)
    vals, idxs = [], []
    for _ in range(TOP_K):
        mx = jnp.max(logits, axis=0, keepdims=True)
        sel = jnp.min(jnp.where(logits == mx, eidx, N_EXPERTS), axis=0, keepdims=True)
        vals.append(mx)
        idxs.append(sel)
        logits = jnp.where(eidx == sel, -jnp.inf, logits)
    ex = [jnp.exp(v - vals[0]) for v in vals]
    inv = 1.0 / (ex[0] + ex[1] + ex[2] + ex[3])
    rows = [e * inv for e in ex] + [lax.bitcast_convert_type(s, jnp.float32) for s in idxs]
    packed = jnp.concatenate(rows + [jnp.zeros((LANES - 2 * TOP_K, tm), jnp.float32)], axis=0)
    route_ref[...] = packed.T


def _outproj_router(x2d, od, osb, mod, g, wo_bf, wr, br, seq):
    n, d = x2d.shape
    tm = min(ROW_TILE, seq)
    per_b = seq // tm
    return pl.pallas_call(
        _outproj_kernel,
        grid=(n // tm,),
        in_specs=[pl.BlockSpec((tm, d), lambda i: (i, 0)),
                  pl.BlockSpec((tm, DIFF_WIDTH), lambda i: (i, 0)),
                  pl.BlockSpec((tm, SB_WIDTH), lambda i: (i, 0)),
                  pl.BlockSpec((1, N_MOD, d), lambda i: (i // per_b, 0, 0)),
                  pl.BlockSpec((1, d), lambda i: (0, 0)),
                  pl.BlockSpec((d, d), lambda i: (0, 0)),
                  pl.BlockSpec((N_EXPERTS, d), lambda i: (0, 0)),
                  pl.BlockSpec((N_EXPERTS, 1), lambda i: (0, 0))],
        out_specs=[pl.BlockSpec((tm, d), lambda i: (i, 0)),
                   pl.BlockSpec((tm * SUBLANES, LANES), lambda i: (i, 0)),
                   pl.BlockSpec((tm, LANES), lambda i: (i, 0))],
        out_shape=[jax.ShapeDtypeStruct((n, d), jnp.float32),
                   jax.ShapeDtypeStruct((n * SUBLANES, LANES), jnp.float32),
                   jax.ShapeDtypeStruct((n, LANES), jnp.float32)],
        compiler_params=_cparams(("arbitrary",)),
        name="outproj_router",
    )(x2d, od, osb, mod, g, wo_bf, wr, br)


def _moe_n_blocks(n_tok):
    n_rows = n_tok * TOP_K
    return -(-(n_rows + N_EXPERTS * (MOE_BLOCK - 1)) // MOE_BLOCK)


N_SPARE = 2 * MOE_BLOCK


def _routing_tables(picks):
    n_tok = picks.shape[0]
    n_rows = n_tok * TOP_K
    n_blocks = _moe_n_blocks(n_tok)
    n_slots = n_blocks * MOE_BLOCK
    n_fill = MOE_BLOCK - 1
    stride = n_rows + MOE_BLOCK
    flat_e = picks.reshape(-1)
    experts = jnp.arange(N_EXPERTS, dtype=jnp.int32)
    counts = jnp.sum(flat_e[None, :] == experts[:, None], axis=1).astype(jnp.int32)
    need = (-counts) % MOE_BLOCK
    real = flat_e * stride + jnp.arange(n_rows, dtype=jnp.int32)
    j = jnp.arange(n_fill, dtype=jnp.int32)
    fill = jnp.where(j[None, :] < need[:, None], experts[:, None] * stride + n_rows + j[None, :],
                     N_EXPERTS * stride).reshape(-1)
    tail = jnp.full((n_slots - n_rows - N_EXPERTS * n_fill,), N_EXPERTS * stride, jnp.int32)
    skeys = lax.sort(jnp.concatenate([real, fill, tail])).reshape(n_blocks, MOE_BLOCK)
    block_e = jnp.minimum(skeys[:, 0] // stride, N_EXPERTS - 1).astype(jnp.int32)
    f = skeys % stride
    valid = jnp.logical_and(f < n_rows, skeys < N_EXPERTS * stride)
    slot = jnp.arange(n_slots, dtype=jnp.int32).reshape(n_blocks, MOE_BLOCK)
    slot_dst = jnp.where(valid, (f % TOP_K) * n_tok + f // TOP_K, n_rows + slot % N_SPARE).astype(jnp.int32)
    slot_tok = jnp.where(valid, f // TOP_K, 0).astype(jnp.int32)
    return block_e, slot_tok.reshape(n_blocks, 1, MOE_BLOCK), slot_dst.reshape(n_blocks, 1, MOE_BLOCK)


def _moe_kernel(be_ref, tok0_ref, tok1_ref, tokn_ref, dst_ref, h_hbm, wgu_ref, bgu_ref, wd_ref, bd_ref, y_hbm,
                xbuf, ybuf, gsem, ssem, *, n_rows):
    i = pl.program_id(0)
    nb = pl.num_programs(0)
    rows = MOE_BLOCK
    nx = GATHER_AHEAD + 1
    xslot = i % nx
    yslot = i % 2

    def start_gather(idx_ref, slot):
        for r in range(rows):
            pltpu.make_async_copy(h_hbm.at[pl.ds(idx_ref[0, 0, r] * SUBLANES, SUBLANES)],
                                  xbuf.at[slot, pl.ds(r * SUBLANES, SUBLANES)], gsem.at[slot]).start()

    def wait_gather(slot):
        pltpu.make_async_copy(h_hbm.at[pl.ds(0, rows * SUBLANES)], xbuf.at[slot], gsem.at[slot]).wait()

    def wait_scatter(slot):
        pltpu.make_async_copy(ybuf.at[slot], y_hbm.at[pl.ds(0, rows * SUBLANES)], ssem.at[slot]).wait()

    @pl.when(i == 0)
    def _():
        start_gather(tok0_ref, 0)
        start_gather(tok1_ref, 1)
        ybuf[1] = jnp.zeros(ybuf.shape[1:], jnp.float32)
        for part in range(N_SPARE // rows):
            pltpu.make_async_copy(ybuf.at[1], y_hbm.at[pl.ds((n_rows + part * rows) * SUBLANES, rows * SUBLANES)],
                                  ssem.at[1]).start()
        for part in range(N_SPARE // rows):
            wait_scatter(1)

    wait_gather(xslot)
    xb = jnp.concatenate([xbuf[xslot, pl.ds(c, rows, stride=SUBLANES), :] for c in range(TOKEN_TILES)],
                         axis=1).astype(jnp.bfloat16)
    gu = jnp.dot(xb, wgu_ref[0], preferred_element_type=jnp.float32) + bgu_ref[0]
    glu = jnp.minimum(gu[:, :D_FF], SWIGLU_LIMIT)
    lin = jnp.clip(gu[:, D_FF:], -SWIGLU_LIMIT, SWIGLU_LIMIT)
    act = glu * jax.nn.sigmoid(SWIGLU_ALPHA * glu) * (lin + 1.0)
    y = jnp.dot(act.astype(jnp.bfloat16), wd_ref[0], preferred_element_type=jnp.float32) + bd_ref[0]

    start_gather(tokn_ref, (i + GATHER_AHEAD) % nx)

    @pl.when(i >= 2)
    def _():
        wait_scatter(yslot)

    for c in range(TOKEN_TILES):
        ybuf[yslot, pl.ds(c, rows, stride=SUBLANES), :] = y[:, c * LANES:(c + 1) * LANES]
    for r in range(rows):
        pltpu.make_async_copy(ybuf.at[yslot, pl.ds(r * SUBLANES, SUBLANES)],
                              y_hbm.at[pl.ds(dst_ref[0, 0, r] * SUBLANES, SUBLANES)], ssem.at[yslot]).start()

    @pl.when(i == nb - 1)
    def _():
        wait_gather((i + 1) % nx)
        wait_gather((i + 2) % nx)
        wait_scatter(yslot)

        @pl.when(nb >= 2)
        def _():
            wait_scatter(1 - yslot)


def _moe(h_tiles, block_e, slot_tok, slot_dst, wgu_bf, bgu, wd_bf, bd):
    n_tok = h_tiles.shape[0] // SUBLANES
    d = D_MODEL
    n_rows = n_tok * TOP_K
    n_blocks = slot_tok.shape[0]
    rows = MOE_BLOCK
    assert n_blocks >= GATHER_AHEAD

    def idx_block(ahead):
        return pl.BlockSpec((1, 1, rows), lambda i, be: (jnp.minimum(i + ahead, n_blocks - 1), 0, 0),
                            memory_space=pltpu.SMEM)

    grid_spec = pltpu.PrefetchScalarGridSpec(
        num_scalar_prefetch=1,
        grid=(n_blocks,),
        in_specs=[idx_block(0), idx_block(1), idx_block(GATHER_AHEAD), idx_block(0),
                  pl.BlockSpec(memory_space=pl.ANY),
                  pl.BlockSpec((1, d, 2 * D_FF), lambda i, be: (be[i], 0, 0)),
                  pl.BlockSpec((1, 1, 2 * D_FF), lambda i, be: (be[i], 0, 0)),
                  pl.BlockSpec((1, D_FF, d), lambda i, be: (be[i], 0, 0)),
                  pl.BlockSpec((1, 1, d), lambda i, be: (be[i], 0, 0))],
        out_specs=pl.BlockSpec(memory_space=pl.ANY),
        scratch_shapes=[pltpu.VMEM((GATHER_AHEAD + 1, rows * SUBLANES, LANES), jnp.float32),
                        pltpu.VMEM((2, rows * SUBLANES, LANES), jnp.float32),
                        pltpu.SemaphoreType.DMA((GATHER_AHEAD + 1,)),
                        pltpu.SemaphoreType.DMA((2,))],
    )
    return pl.pallas_call(
        functools.partial(_moe_kernel, n_rows=n_rows),
        grid_spec=grid_spec,
        out_shape=jax.ShapeDtypeStruct(((n_rows + N_SPARE) * SUBLANES, LANES), jnp.float32),
        compiler_params=_cparams(("arbitrary",)),
        name="moe_experts",
    )(block_e, slot_tok, slot_tok, slot_tok, slot_dst, h_tiles, wgu_bf, bgu, wd_bf, bd)


def _combine_kernel(x_ref, y0_ref, y1_ref, y2_ref, y3_ref, gate_ref, mod_ref, g_ref, o_ref, *, final):
    tm = x_ref.shape[0]
    y_refs = (y0_ref, y1_ref, y2_ref, y3_ref)
    gk = [jnp.broadcast_to(gate_ref[:, k:k + 1], (tm, LANES)) for k in range(TOP_K)]
    parts = []
    for c in range(TOKEN_TILES):
        acc = gk[0] * y_refs[0][pl.ds(c, tm, stride=SUBLANES), :]
        for k in range(1, TOP_K):
            acc = acc + gk[k] * y_refs[k][pl.ds(c, tm, stride=SUBLANES), :]
        parts.append(acc)
    x2 = x_ref[...] + mod_ref[0, 5:6, :] * jnp.concatenate(parts, axis=1)
    if final:
        ms = jnp.mean(x2 * x2, axis=-1, keepdims=True)
        x2 = x2 * lax.rsqrt(ms + NORM_EPS) * g_ref[...]
    o_ref[...] = x2


def _combine(x1, y_tiles, gates, mod, g, seq, final):
    n, d = x1.shape
    tm = min(COMBINE_TILE, seq)
    per_b = seq // tm
    plane_blocks = n // tm
    assert n % tm == 0 and TOP_K == 4

    def plane_spec(k):
        return pl.BlockSpec((tm * SUBLANES, LANES), lambda i: (k * plane_blocks + i, 0))

    return pl.pallas_call(
        functools.partial(_combine_kernel, final=final),
        grid=(n // tm,),
        in_specs=[pl.BlockSpec((tm, d), lambda i: (i, 0)),
                  plane_spec(0), plane_spec(1), plane_spec(2), plane_spec(3),
                  pl.BlockSpec((tm, LANES), lambda i: (i, 0)),
                  pl.BlockSpec((1, N_MOD, d), lambda i: (i // per_b, 0, 0)),
                  pl.BlockSpec((1, d), lambda i: (0, 0))],
        out_specs=pl.BlockSpec((tm, d), lambda i: (i, 0)),
        out_shape=jax.ShapeDtypeStruct((n, d), jnp.float32),
        compiler_params=_cparams(("arbitrary",)),
        name="combine",
    )(x1, y_tiles, y_tiles, y_tiles, y_tiles, gates, mod, g)


def kernel(x, c, w_in, w_out, norm_mix, norm_ffn, w_ada, b_ada, lam_qk, diff_subln, sb_norm, rel_bias,
           w_router, b_router, w_gate_up, b_gate_up, w_down, b_down, final_norm):
    b, s, d = x.shape
    depth = w_in.shape[0]
    assert d == D_MODEL and s % ATT_TQ == 0
    n_tok = b * s

    mod_all = _ada_mod(c, w_ada, b_ada).reshape(depth, b, N_MOD, d)

    q_scale = HEAD_DIM ** -0.5 * LOG2E
    colscale = jnp.concatenate([
        jnp.full((DIFF_WIDTH,), q_scale, jnp.float32), jnp.ones((2 * DIFF_WIDTH,), jnp.float32),
        jnp.full((SB_WIDTH,), -q_scale, jnp.float32), jnp.ones((2 * SB_WIDTH,), jnp.float32)]).reshape(1, PROJ_WIDTH)
    bias_tiles, bias_top, bias_spread = _diff_bias_tiles(rel_bias)
    kk = jnp.arange(SB_SUB)
    tri = (kk[:, None] >= kk[None, :]).astype(jnp.bfloat16)

    xf = x.reshape(n_tok, d)
    for l in range(depth):
        mod = mod_all[l]
        lam_init = 0.8 - 0.6 * math.exp(-0.3 * l)
        lq = lam_qk[l].astype(jnp.float32)
        lam = jnp.exp(jnp.sum(lq[0] * lq[1])) - jnp.exp(jnp.sum(lq[2] * lq[3])) + lam_init
        aux = jnp.broadcast_to(jnp.stack([lam, bias_top, bias_spread] + [jnp.float32(0.0)] * (SUBLANES - 3))[:, None],
                               (SUBLANES, LANES)).astype(jnp.float32)

        proj = _inproj(xf.reshape(b, s, d), mod, norm_mix[l].reshape(1, d), w_in[l].astype(jnp.bfloat16), colscale)
        o_diff = _diff_attention(proj, bias_tiles, aux, diff_subln[l].reshape(1, LANES), 1.0 - lam_init)
        o_sb = _sb_attention(proj, tri, jnp.tile(sb_norm[l], 2).reshape(1, LANES))

        x1, h_tiles, route = _outproj_router(
            xf, o_diff.reshape(n_tok, DIFF_WIDTH), o_sb.reshape(n_tok, SB_WIDTH), mod,
            norm_ffn[l].reshape(1, d), w_out[l].astype(jnp.bfloat16),
            w_router[l].T, b_router[l].reshape(N_EXPERTS, 1), s)
        picks = lax.bitcast_convert_type(route[:, TOP_K:2 * TOP_K], jnp.int32)

        block_e, slot_tok, slot_dst = _routing_tables(picks)
        y_tiles = _moe(h_tiles, block_e, slot_tok, slot_dst,
                       w_gate_up[l].astype(jnp.bfloat16), b_gate_up[l].reshape(N_EXPERTS, 1, 2 * D_FF),
                       w_down[l].astype(jnp.bfloat16), b_down[l].reshape(N_EXPERTS, 1, d))
        xf = _combine(x1, y_tiles, route, mod, final_norm.reshape(1, d), s, final=(l == depth - 1))
    return xf.reshape(b, s, d)
```

```python
import functools
import math

import jax
import jax.numpy as jnp
from jax import lax
from jax.experimental import pallas as pl
from jax.experimental.pallas import tpu as pltpu

D_MODEL = 1024
HEAD_DIM = 64
CHUNK = 64
DIFF_WIDTH = 512
N_DIFF_HEADS = 4
SB_WIDTH = 512
N_SB_HEADS = 8
PROJ_WIDTH = 3 * DIFF_WIDTH + 3 * SB_WIDTH
N_BUCKETS = 32
MAX_DISTANCE = 128
N_EXPERTS = 32
TOP_K = 4
D_FF = D_MODEL
SWIGLU_LIMIT = 7.0
SWIGLU_ALPHA = 1.702
MOE_BLOCK = 256
NORM_EPS = 1e-5
N_MOD = 6

LANES = 128
SUBLANES = 8
TOKEN_TILES = D_MODEL // LANES
N_COL_BLOCKS = PROJ_WIDTH // LANES
ATT_TQ = 512
ATT_TK = 512
SB_SUB = 256
SB_NSUB = 2
SB_DEAD_LOG2 = -192.0
GATHER_AHEAD = 2
ROW_TILE = 512
COMBINE_TILE = 256
MASK_VALUE = -1e30
LOG2E = math.log2(math.e)
VMEM_LIMIT = 56 * 1024 * 1024

_NT_DIMS = (((1,), (1,)), ((), ()))

assert TOKEN_TILES == SUBLANES and ATT_TQ == ATT_TK and ATT_TQ % (SB_SUB * SB_NSUB) == 0


def _cparams(sem):
    return pltpu.CompilerParams(dimension_semantics=sem, vmem_limit_bytes=VMEM_LIMIT)


def _ada_kernel(c_ref, w_ref, b_ref, o_ref):
    c = c_ref[...]
    cond = c * jax.nn.sigmoid(c)
    o_ref[0] = jnp.dot(cond, w_ref[0], preferred_element_type=jnp.float32,
                       precision=lax.Precision.HIGHEST) + b_ref[0]


def _ada_mod(c, w_ada, b_ada):
    depth, d, n = w_ada.shape
    b = c.shape[0]
    tn = 1536
    return pl.pallas_call(
        _ada_kernel,
        grid=(depth, n // tn),
        in_specs=[pl.BlockSpec((b, d), lambda l, j: (0, 0)),
                  pl.BlockSpec((1, d, tn), lambda l, j: (l, 0, j)),
                  pl.BlockSpec((1, 1, tn), lambda l, j: (l, 0, j))],
        out_specs=pl.BlockSpec((1, b, tn), lambda l, j: (l, 0, j)),
        out_shape=jax.ShapeDtypeStruct((depth, b, n), jnp.float32),
        compiler_params=_cparams(("arbitrary", "arbitrary")),
        name="ada_mod",
    )(c, w_ada, b_ada.reshape(depth, 1, n))


def _inproj_kernel(x_ref, mod_ref, g_ref, w_ref, cs_ref, o_ref):
    x = x_ref[0]
    ms = jnp.mean(x * x, axis=-1, keepdims=True)
    y = x * lax.rsqrt(ms + NORM_EPS) * g_ref[...]
    h = y * (1.0 + mod_ref[0, 1:2, :]) + mod_ref[0, 0:1, :]
    hb = h.astype(jnp.bfloat16)
    nchunk = 512
    for c in range(PROJ_WIDTH // nchunk):
        acc = jnp.dot(hb, w_ref[:, c * nchunk:(c + 1) * nchunk], preferred_element_type=jnp.float32)
        acc = acc * cs_ref[:, c * nchunk:(c + 1) * nchunk]
        for j in range(nchunk // LANES):
            o_ref[0, c * (nchunk // LANES) + j] = acc[:, j * LANES:(j + 1) * LANES].astype(jnp.bfloat16)


def _inproj(x, mod, g, w_bf, colscale):
    b, s, d = x.shape
    tm = min(ROW_TILE, s)
    return pl.pallas_call(
        _inproj_kernel,
        grid=(b, s // tm),
        in_specs=[pl.BlockSpec((1, tm, d), lambda bi, i: (bi, i, 0)),
                  pl.BlockSpec((1, N_MOD, d), lambda bi, i: (bi, 0, 0)),
                  pl.BlockSpec((1, d), lambda bi, i: (0, 0)),
                  pl.BlockSpec((d, PROJ_WIDTH), lambda bi, i: (0, 0)),
                  pl.BlockSpec((1, PROJ_WIDTH), lambda bi, i: (0, 0))],
        out_specs=pl.BlockSpec((1, N_COL_BLOCKS, tm, LANES), lambda bi, i: (bi, 0, i, 0)),
        out_shape=jax.ShapeDtypeStruct((b, N_COL_BLOCKS, s, LANES), jnp.bfloat16),
        compiler_params=_cparams(("arbitrary", "arbitrary")),
        name="inproj",
    )(x, mod, g, w_bf, colscale)


def _split_halves(q):
    lane = lax.broadcasted_iota(jnp.int32, q.shape, 1)
    zero = jnp.zeros_like(q)
    return jnp.concatenate([jnp.where(lane < HEAD_DIM, q, zero), jnp.where(lane >= HEAD_DIM, q, zero)], axis=0)


def _lane_tile(x, n):
    return x if n == 1 else jnp.concatenate([x] * n, axis=1)


DIFF_SHIFT_MARGIN = 1.01
DIFF_SHIFT_SPAN = 100.0


def _diff_kernel(q_ref, k_ref, v_ref, bias_ref, aux_ref, g_ref, o_ref, m_ref, l_ref, acc_ref, kn_ref, *, out_scale):
    i = pl.program_id(2)
    tq, tk = ATT_TQ, ATT_TK
    seq = k_ref.shape[2]
    q2 = _split_halves(q_ref[0, 0])
    first = lax.broadcasted_iota(jnp.int32, (tk, LANES), 1) < HEAD_DIM

    @pl.when(i == 0)
    def _():
        def chunk(c, mx):
            kc = k_ref[0, 0, pl.ds(pl.multiple_of(c * tk, tk), tk), :].astype(jnp.float32)
            sq = kc * kc
            lo = jnp.max(jnp.sum(jnp.where(first, sq, 0.0), axis=1, keepdims=True), axis=0, keepdims=True)
            hi = jnp.max(jnp.sum(jnp.where(first, 0.0, sq), axis=1, keepdims=True), axis=0, keepdims=True)
            return jnp.maximum(mx[0], lo), jnp.maximum(mx[1], hi)
        zero = jnp.zeros((1, 1), jnp.float32)
        lo, hi = lax.fori_loop(0, seq // tk, chunk, (zero, zero))
        kn_ref[0:1, :] = jnp.broadcast_to(jnp.sqrt(lo), (1, LANES))
        kn_ref[1:2, :] = jnp.broadcast_to(jnp.sqrt(hi), (1, LANES))

    qf = q2.astype(jnp.float32)
    qn = jnp.sqrt(jnp.sum(qf * qf, axis=1, keepdims=True))
    bound = jnp.concatenate([qn[:tq] * kn_ref[0:1, :], qn[tq:] * kn_ref[1:2, :]], axis=0) * DIFF_SHIFT_MARGIN
    fixed_ok = jnp.max(2.0 * bound + aux_ref[2:3, :]) <= DIFF_SHIFT_SPAN

    l_ref[...] = jnp.zeros(l_ref.shape, jnp.float32)
    acc_ref[...] = jnp.zeros(acc_ref.shape, jnp.float32)

    def scores(j, bias):
        k = k_ref[0, 0, pl.ds(pl.multiple_of(j * tk, tk), tk), :]
        s = lax.dot_general(q2, k, _NT_DIMS, preferred_element_type=jnp.float32)
        return s if bias is None else s + jnp.concatenate([bias, bias], axis=0)

    def sweep(update, unroll):
        update(i, bias_ref[0, :, tk:2 * tk])

        @pl.when(i >= 1)
        def _():
            update(i - 1, bias_ref[0, :, 0:tk])

        n_far = jnp.maximum(i - 1, 0)

        def far(t, carry):
            for u in range(unroll):
                update(unroll * t + u, None)
            return carry
        lax.fori_loop(0, n_far // unroll, far, 0)

        def rest(j, carry):
            update(j, None)
            return carry
        lax.fori_loop(n_far - n_far % unroll, n_far, rest, 0)

    @pl.when(fixed_ok)
    def _():
        m_ref[...] = bound + aux_ref[1:2, :]

        def update(j, bias):
            v = v_ref[0, 0, pl.ds(pl.multiple_of(j * tk, tk), tk), :]
            p = jnp.exp2(scores(j, bias) - _lane_tile(m_ref[...], tk // LANES))
            part = p[:, 0:LANES]
            for c in range(1, tk // LANES):
                part = part + p[:, c * LANES:(c + 1) * LANES]
            l_ref[...] += part
            acc_ref[...] += jnp.dot(p.astype(jnp.bfloat16), v, preferred_element_type=jnp.float32)

        sweep(update, 2)
        l_ref[...] = jnp.broadcast_to(jnp.sum(l_ref[...], axis=1, keepdims=True), l_ref.shape)

    @pl.when(jnp.logical_not(fixed_ok))
    def _():
        m_ref[...] = jnp.full(m_ref.shape, -jnp.inf, jnp.float32)

        def update(j, bias):
            v = v_ref[0, 0, pl.ds(pl.multiple_of(j * tk, tk), tk), :]
            s = scores(j, bias)
            m_prev = m_ref[...]
            m_next = jnp.maximum(m_prev, jnp.max(s, axis=1, keepdims=True))
            p = jnp.exp2(s - _lane_tile(m_next, tk // LANES))
            alpha = jnp.exp2(m_prev - m_next)
            l_ref[...] = alpha * l_ref[...] + jnp.sum(p, axis=1, keepdims=True)
            acc_ref[...] = alpha * acc_ref[...] + jnp.dot(p.astype(jnp.bfloat16), v,
                                                          preferred_element_type=jnp.float32)
            m_ref[...] = m_next

        sweep(update, 1)

    o = acc_ref[...] * (1.0 / l_ref[...])
    od = o[:tq] - aux_ref[0:1, :] * o[tq:]
    ms = jnp.mean(od * od, axis=-1, keepdims=True)
    y = od * lax.rsqrt(ms + NORM_EPS) * g_ref[...] * out_scale
    o_ref[0] = y.astype(jnp.bfloat16)


def _diff_attention(proj, bias, aux, subln, out_scale):
    b, _, s, _ = proj.shape
    tq = ATT_TQ
    kern = functools.partial(_diff_kernel, out_scale=out_scale)
    return pl.pallas_call(
        kern,
        grid=(b, N_DIFF_HEADS, s // tq),
        in_specs=[pl.BlockSpec((1, 1, tq, LANES), lambda bi, h, i: (bi, h, i, 0)),
                  pl.BlockSpec((1, 1, s, LANES), lambda bi, h, i: (bi, N_DIFF_HEADS + h, 0, 0)),
                  pl.BlockSpec((1, 1, s, LANES), lambda bi, h, i: (bi, 2 * N_DIFF_HEADS + h, 0, 0)),
                  pl.BlockSpec((1, tq, 2 * ATT_TK), lambda bi, h, i: (h, 0, 0)),
                  pl.BlockSpec((SUBLANES, LANES), lambda bi, h, i: (0, 0)),
                  pl.BlockSpec((1, LANES), lambda bi, h, i: (0, 0))],
        out_specs=pl.BlockSpec((1, tq, LANES), lambda bi, h, i: (bi, i, h)),
        out_shape=jax.ShapeDtypeStruct((b, s, DIFF_WIDTH), jnp.bfloat16),
        scratch_shapes=[pltpu.VMEM((2 * tq, LANES), jnp.float32),
                        pltpu.VMEM((2 * tq, LANES), jnp.float32),
                        pltpu.VMEM((2 * tq, LANES), jnp.float32),
                        pltpu.VMEM((SUBLANES, LANES), jnp.float32)],
        compiler_params=_cparams(("arbitrary", "arbitrary", "arbitrary")),
        name="diff_attn",
    )(proj, proj, proj, bias, aux, subln)


def _t5_bucket(rel):
    half = N_BUCKETS // 2
    max_exact = half // 2
    n = jnp.abs(rel)
    nf = jnp.maximum(n, 1).astype(jnp.float32)
    large = max_exact + (jnp.log(nf / max_exact) / math.log(MAX_DISTANCE / max_exact)
                         * (half - max_exact)).astype(jnp.int32)
    large = jnp.minimum(large, half - 1)
    return jnp.where(rel > 0, half, 0) + jnp.where(n < max_exact, n, large)


def _diff_bias_tiles(rel_bias):
    tq, tk = ATT_TQ, ATT_TK
    span = 2 * tk + tq
    rel = jnp.arange(span, dtype=jnp.int32) - (tk + tq - 1)
    far_bucket = _t5_bucket(jnp.full((1,), -(tk + 1), jnp.int32))[0]
    rb = rel_bias.astype(jnp.float32)
    by_rel = ((rb[_t5_bucket(rel)] - rb[far_bucket]) * LOG2E).T
    nh = by_rel.shape[0]
    shifted = jnp.tile(by_rel, (1, tq))[:, :tq * (span - 1)].reshape(nh, tq, span - 1)
    bias = shifted[:, :, tq - 1:tq - 1 + 2 * tk]
    qpos = jnp.arange(tq)[:, None]
    kpos = jnp.arange(-tk, tk)[None, :]
    allowed = (kpos // CHUNK) <= (qpos // CHUNK)
    top = jnp.maximum(jnp.max(by_rel), 0.0)
    spread = top - jnp.minimum(jnp.min(by_rel), 0.0)
    return jnp.where(allowed[None], bias, MASK_VALUE), top, spread


def _sb_kernel(q_ref, k_ref, v_ref, tri_ref, g_ref, o_ref, carry_ref, acc_ref):
    i = pl.program_id(2)
    tq, sub = ATT_TQ, SB_SUB
    q2 = _split_halves(q_ref[0, 0])
    tri = tri_ref[...]
    sign = jnp.uint32(0x80000000)

    carry_ref[...] = jnp.zeros(carry_ref.shape, jnp.float32)
    acc_ref[...] = jnp.zeros(acc_ref.shape, jnp.float32)

    def group(jb, masked, nsub):
        carry = carry_ref[...]
        a_parts, v_parts = [], []
        for t in range(nsub):
            off = pl.multiple_of((jb - t) * sub, sub)
            k = k_ref[0, 0, pl.ds(off, sub), :]
            v_parts.append(v_ref[0, 0, pl.ds(off, sub), :])
            u = lax.dot_general(q2, k, _NT_DIMS, preferred_element_type=jnp.float32)
            nabs = lax.bitcast_convert_type(lax.bitcast_convert_type(u, jnp.uint32) | sign, jnp.float32)
            lf = jnp.minimum(u, 0.0) - jnp.log2(1.0 + jnp.exp2(nabs))
            if masked:
                row = lax.broadcasted_iota(jnp.int32, (tq, sub), 0) + i * tq
                col = lax.broadcasted_iota(jnp.int32, (tq, sub), 1) + (jb - t) * sub
                strict = jnp.concatenate([col < row, col < row], axis=0)
                lf = jnp.where(strict, lf, 0.0)
            cum = jnp.dot(lf.astype(jnp.bfloat16), tri, preferred_element_type=jnp.float32)
            a = jnp.exp2(cum - u + _lane_tile(carry, sub // LANES))
            if masked:
                a = jnp.where(strict, a, 0.0)
            a_parts.append(a.astype(jnp.bfloat16))
            carry = carry + jnp.sum(lf, axis=1, keepdims=True)
        acc_ref[...] += jnp.dot(jnp.concatenate(a_parts, axis=1), jnp.concatenate(v_parts, axis=0),
                                preferred_element_type=jnp.float32)
        carry_ref[...] = carry

    nd = tq // sub
    for g in range(nd // SB_NSUB):
        group((i + 1) * nd - 1 - g * SB_NSUB, True, SB_NSUB)

    n_far = i * nd

    def alive(state):
        t, top = state
        return jnp.logical_and(t < n_far, top > SB_DEAD_LOG2)

    def far(state):
        t, _ = state
        group(i * nd - 1 - t, False, 1)
        return t + 1, jnp.max(carry_ref[...])

    lax.while_loop(alive, far, (jnp.int32(0), jnp.max(carry_ref[...])))

    acc = acc_ref[...]
    lane = lax.broadcasted_iota(jnp.int32, (tq, LANES), 1)
    first = lane < HEAD_DIM
    o = jnp.where(first, acc[:tq], acc[tq:])
    sq = o * o
    ss_a = jnp.sum(jnp.where(first, sq, 0.0), axis=-1, keepdims=True)
    ss_b = jnp.sum(jnp.where(first, 0.0, sq), axis=-1, keepdims=True)
    ms = jnp.where(first, ss_a, ss_b) * (1.0 / HEAD_DIM)
    o_ref[0] = (o * lax.rsqrt(ms + NORM_EPS) * g_ref[...]).astype(jnp.bfloat16)


def _sb_attention(proj, tri, gain2):
    b, _, s, _ = proj.shape
    tq = ATT_TQ
    npair = N_SB_HEADS // 2
    base = 3 * N_DIFF_HEADS
    return pl.pallas_call(
        _sb_kernel,
        grid=(b, npair, s // tq),
        in_specs=[pl.BlockSpec((1, 1, tq, LANES), lambda bi, h, i: (bi, base + h, i, 0)),
                  pl.BlockSpec((1, 1, s, LANES), lambda bi, h, i: (bi, base + npair + h, 0, 0)),
                  pl.BlockSpec((1, 1, s, LANES), lambda bi, h, i: (bi, base + 2 * npair + h, 0, 0)),
                  pl.BlockSpec((SB_SUB, SB_SUB), lambda bi, h, i: (0, 0)),
                  pl.BlockSpec((1, LANES), lambda bi, h, i: (0, 0))],
        out_specs=pl.BlockSpec((1, tq, LANES), lambda bi, h, i: (bi, i, h)),
        out_shape=jax.ShapeDtypeStruct((b, s, SB_WIDTH), jnp.bfloat16),
        scratch_shapes=[pltpu.VMEM((2 * tq, LANES), jnp.float32),
                        pltpu.VMEM((2 * tq, LANES), jnp.float32)],
        compiler_params=_cparams(("arbitrary", "arbitrary", "arbitrary")),
        name="sb_attn",
    )(proj, proj, proj, tri, gain2)


def _outproj_kernel(x_ref, od_ref, os_ref, mod_ref, g_ref, wo_ref, wr_ref, br_ref,
                    x1_ref, h_ref, route_ref, pick_ref):
    tm = x_ref.shape[0]
    y = jnp.dot(od_ref[...], wo_ref[0:DIFF_WIDTH, :], preferred_element_type=jnp.float32)
    y = y + jnp.dot(os_ref[...], wo_ref[DIFF_WIDTH:, :], preferred_element_type=jnp.float32)
    x1 = x_ref[...] + mod_ref[0, 2:3, :] * y
    x1_ref[...] = x1
    ms = jnp.mean(x1 * x1, axis=-1, keepdims=True)
    h = x1 * lax.rsqrt(ms + NORM_EPS) * g_ref[...]
    h = h * (1.0 + mod_ref[0, 4:5, :]) + mod_ref[0, 3:4, :]
    for c in range(TOKEN_TILES):
        h_ref[pl.ds(c, tm, stride=SUBLANES), :] = h[:, c * LANES:(c + 1) * LANES]
    logits = lax.dot_general(wr_ref[...], h, _NT_DIMS, preferred_element_type=jnp.float32,
                             precision=lax.Precision.HIGHEST) + br_ref[...]
    eidx = lax.broadcasted_iota(jnp.int32, logits.shape, 0)
    vals, idxs = [], []
    for _ in range(TOP_K):
        mx = jnp.max(logits, axis=0, keepdims=True)
        sel = jnp.min(jnp.where(logits == mx, eidx, N_EXPERTS), axis=0, keepdims=True)
        vals.append(mx)
        idxs.append(sel)
        logits = jnp.where(eidx == sel, -jnp.inf, logits)
    ex = [jnp.exp(v - vals[0]) for v in vals]
    inv = 1.0 / (ex[0] + ex[1] + ex[2] + ex[3])
    pick_ref[...] = jnp.concatenate(idxs, axis=0)
    packed = jnp.concatenate([e * inv for e in ex] + [jnp.zeros((LANES - TOP_K, tm), jnp.float32)], axis=0)
    route_ref[...] = packed.T


def _outproj_router(x2d, od, osb, mod, g, wo_bf, wr, br, seq):
    n, d = x2d.shape
    tm = min(ROW_TILE, seq)
    per_b = seq // tm
    return pl.pallas_call(
        _outproj_kernel,
        grid=(n // tm,),
        in_specs=[pl.BlockSpec((tm, d), lambda i: (i, 0)),
                  pl.BlockSpec((tm, DIFF_WIDTH), lambda i: (i, 0)),
                  pl.BlockSpec((tm, SB_WIDTH), lambda i: (i, 0)),
                  pl.BlockSpec((1, N_MOD, d), lambda i: (i // per_b, 0, 0)),
                  pl.BlockSpec((1, d), lambda i: (0, 0)),
                  pl.BlockSpec((d, d), lambda i: (0, 0)),
                  pl.BlockSpec((N_EXPERTS, d), lambda i: (0, 0)),
                  pl.BlockSpec((N_EXPERTS, 1), lambda i: (0, 0))],
        out_specs=[pl.BlockSpec((tm, d), lambda i: (i, 0)),
                   pl.BlockSpec((tm * SUBLANES, LANES), lambda i: (i, 0)),
                   pl.BlockSpec((tm, LANES), lambda i: (i, 0)),
                   pl.BlockSpec((TOP_K, tm), lambda i: (0, i))],
        out_shape=[jax.ShapeDtypeStruct((n, d), jnp.float32),
                   jax.ShapeDtypeStruct((n * SUBLANES, LANES), jnp.float32),
                   jax.ShapeDtypeStruct((n, LANES), jnp.float32),
                   jax.ShapeDtypeStruct((TOP_K, n), jnp.int32)],
        compiler_params=_cparams(("arbitrary",)),
        name="outproj_router",
    )(x2d, od, osb, mod, g, wo_bf, wr, br)


def _moe_n_blocks(n_tok):
    n_rows = n_tok * TOP_K
    return -(-(n_rows + N_EXPERTS * (MOE_BLOCK - 1)) // MOE_BLOCK)


N_SPARE = 2 * MOE_BLOCK


def _routing_tables(picks):
    n_tok = picks.shape[1]
    n_rows = n_tok * TOP_K
    n_blocks = _moe_n_blocks(n_tok)
    n_slots = n_blocks * MOE_BLOCK
    n_fill = MOE_BLOCK - 1
    stride = n_rows + MOE_BLOCK
    flat_e = picks.reshape(-1)
    experts = jnp.arange(N_EXPERTS, dtype=jnp.int32)
    counts = jnp.sum(flat_e[None, :] == experts[:, None], axis=1).astype(jnp.int32)
    need = (-counts) % MOE_BLOCK
    real = flat_e * stride + jnp.arange(n_rows, dtype=jnp.int32)
    j = jnp.arange(n_fill, dtype=jnp.int32)
    fill = jnp.where(j[None, :] < need[:, None], experts[:, None] * stride + n_rows + j[None, :],
                     N_EXPERTS * stride).reshape(-1)
    tail = jnp.full((n_slots - n_rows - N_EXPERTS * n_fill,), N_EXPERTS * stride, jnp.int32)
    skeys = lax.sort(jnp.concatenate([real, fill, tail])).reshape(n_blocks, MOE_BLOCK)
    block_e = jnp.minimum(skeys[:, 0] // stride, N_EXPERTS - 1).astype(jnp.int32)
    f = skeys % stride
    valid = jnp.logical_and(f < n_rows, skeys < N_EXPERTS * stride)
    slot = jnp.arange(n_slots, dtype=jnp.int32).reshape(n_blocks, MOE_BLOCK)
    slot_dst = jnp.where(valid, f, n_rows + slot % N_SPARE).astype(jnp.int32)
    slot_tok = jnp.where(valid, f % n_tok, 0).astype(jnp.int32)
    return block_e, slot_tok.reshape(n_blocks, 1, MOE_BLOCK), slot_dst.reshape(n_blocks, 1, MOE_BLOCK)


def _moe_kernel(be_ref, tok0_ref, tok1_ref, tokn_ref, dst_ref, h_hbm, wgu_ref, bgu_ref, wd_ref, bd_ref, y_hbm,
                xbuf, ybuf, gsem, ssem, *, n_rows):
    i = pl.program_id(0)
    nb = pl.num_programs(0)
    rows = MOE_BLOCK
    nx = GATHER_AHEAD + 1
    xslot = i % nx
    yslot = i % 2

    def start_gather(idx_ref, slot):
        for r in range(rows):
            pltpu.make_async_copy(h_hbm.at[pl.ds(idx_ref[0, 0, r] * SUBLANES, SUBLANES)],
                                  xbuf.at[slot, pl.ds(r * SUBLANES, SUBLANES)], gsem.at[slot]).start()

    def wait_gather(slot):
        pltpu.make_async_copy(h_hbm.at[pl.ds(0, rows * SUBLANES)], xbuf.at[slot], gsem.at[slot]).wait()

    def wait_scatter(slot):
        pltpu.make_async_copy(ybuf.at[slot], y_hbm.at[pl.ds(0, rows * SUBLANES)], ssem.at[slot]).wait()

    @pl.when(i == 0)
    def _():
        start_gather(tok0_ref, 0)
        start_gather(tok1_ref, 1)
        ybuf[1] = jnp.zeros(ybuf.shape[1:], jnp.float32)
        for part in range(N_SPARE // rows):
            pltpu.make_async_copy(ybuf.at[1], y_hbm.at[pl.ds((n_rows + part * rows) * SUBLANES, rows * SUBLANES)],
                                  ssem.at[1]).start()
        for part in range(N_SPARE // rows):
            wait_scatter(1)

    wait_gather(xslot)
    xb = jnp.concatenate([xbuf[xslot, pl.ds(c, rows, stride=SUBLANES), :] for c in range(TOKEN_TILES)],
                         axis=1).astype(jnp.bfloat16)
    gu = jnp.dot(xb, wgu_ref[0], preferred_element_type=jnp.float32) + bgu_ref[0]
    glu = jnp.minimum(gu[:, :D_FF], SWIGLU_LIMIT)
    lin = jnp.clip(gu[:, D_FF:], -SWIGLU_LIMIT, SWIGLU_LIMIT)
    act = glu * jax.nn.sigmoid(SWIGLU_ALPHA * glu) * (lin + 1.0)
    y = jnp.dot(act.astype(jnp.bfloat16), wd_ref[0], preferred_element_type=jnp.float32) + bd_ref[0]

    start_gather(tokn_ref, (i + GATHER_AHEAD) % nx)

    @pl.when(i >= 2)
    def _():
        wait_scatter(yslot)

    for c in range(TOKEN_TILES):
        ybuf[yslot, pl.ds(c, rows, stride=SUBLANES), :] = y[:, c * LANES:(c + 1) * LANES]
    for r in range(rows):
        pltpu.make_async_copy(ybuf.at[yslot, pl.ds(r * SUBLANES, SUBLANES)],
                              y_hbm.at[pl.ds(dst_ref[0, 0, r] * SUBLANES, SUBLANES)], ssem.at[yslot]).start()

    @pl.when(i == nb - 1)
    def _():
        wait_gather((i + 1) % nx)
        wait_gather((i + 2) % nx)
        wait_scatter(yslot)

        @pl.when(nb >= 2)
        def _():
            wait_scatter(1 - yslot)


def _moe(h_tiles, block_e, slot_tok, slot_dst, wgu_bf, bgu, wd_bf, bd):
    n_tok = h_tiles.shape[0] // SUBLANES
    d = D_MODEL
    n_rows = n_tok * TOP_K
    n_blocks = slot_tok.shape[0]
    rows = MOE_BLOCK
    assert n_blocks >= GATHER_AHEAD

    def idx_block(ahead):
        return pl.BlockSpec((1, 1, rows), lambda i, be: (jnp.minimum(i + ahead, n_blocks - 1), 0, 0),
                            memory_space=pltpu.SMEM)

    grid_spec = pltpu.PrefetchScalarGridSpec(
        num_scalar_prefetch=1,
        grid=(n_blocks,),
        in_specs=[idx_block(0), idx_block(1), idx_block(GATHER_AHEAD), idx_block(0),
                  pl.BlockSpec(memory_space=pl.ANY),
                  pl.BlockSpec((1, d, 2 * D_FF), lambda i, be: (be[i], 0, 0)),
                  pl.BlockSpec((1, 1, 2 * D_FF), lambda i, be: (be[i], 0, 0)),
                  pl.BlockSpec((1, D_FF, d), lambda i, be: (be[i], 0, 0)),
                  pl.BlockSpec((1, 1, d), lambda i, be: (be[i], 0, 0))],
        out_specs=pl.BlockSpec(memory_space=pl.ANY),
        scratch_shapes=[pltpu.VMEM((GATHER_AHEAD + 1, rows * SUBLANES, LANES), jnp.float32),
                        pltpu.VMEM((2, rows * SUBLANES, LANES), jnp.float32),
                        pltpu.SemaphoreType.DMA((GATHER_AHEAD + 1,)),
                        pltpu.SemaphoreType.DMA((2,))],
    )
    return pl.pallas_call(
        functools.partial(_moe_kernel, n_rows=n_rows),
        grid_spec=grid_spec,
        out_shape=jax.ShapeDtypeStruct(((n_rows + N_SPARE) * SUBLANES, LANES), jnp.float32),
        compiler_params=_cparams(("arbitrary",)),
        name="moe_experts",
    )(block_e, slot_tok, slot_tok, slot_tok, slot_dst, h_tiles, wgu_bf, bgu, wd_bf, bd)


def _combine_kernel(x_ref, y0_ref, y1_ref, y2_ref, y3_ref, gate_ref, mod_ref, g_ref, o_ref, *, final):
    tm = x_ref.shape[0]
    y_refs = (y0_ref, y1_ref, y2_ref, y3_ref)
    gk = [jnp.broadcast_to(gate_ref[:, k:k + 1], (tm, LANES)) for k in range(TOP_K)]
    parts = []
    for c in range(TOKEN_TILES):
        acc = gk[0] * y_refs[0][pl.ds(c, tm, stride=SUBLANES), :]
        for k in range(1, TOP_K):
            acc = acc + gk[k] * y_refs[k][pl.ds(c, tm, stride=SUBLANES), :]
        parts.append(acc)
    x2 = x_ref[...] + mod_ref[0, 5:6, :] * jnp.concatenate(parts, axis=1)
    if final:
        ms = jnp.mean(x2 * x2, axis=-1, keepdims=True)
        x2 = x2 * lax.rsqrt(ms + NORM_EPS) * g_ref[...]
    o_ref[...] = x2


def _combine(x1, y_tiles, gates, mod, g, seq, final):
    n, d = x1.shape
    tm = min(COMBINE_TILE, seq)
    per_b = seq // tm
    plane_blocks = n // tm
    assert n % tm == 0 and TOP_K == 4

    def plane_spec(k):
        return pl.BlockSpec((tm * SUBLANES, LANES), lambda i: (k * plane_blocks + i, 0))

    return pl.pallas_call(
        functools.partial(_combine_kernel, final=final),
        grid=(n // tm,),
        in_specs=[pl.BlockSpec((tm, d), lambda i: (i, 0)),
                  plane_spec(0), plane_spec(1), plane_spec(2), plane_spec(3),
                  pl.BlockSpec((tm, LANES), lambda i: (i, 0)),
                  pl.BlockSpec((1, N_MOD, d), lambda i: (i // per_b, 0, 0)),
                  pl.BlockSpec((1, d), lambda i: (0, 0))],
        out_specs=pl.BlockSpec((tm, d), lambda i: (i, 0)),
        out_shape=jax.ShapeDtypeStruct((n, d), jnp.float32),
        compiler_params=_cparams(("arbitrary",)),
        name="combine",
    )(x1, y_tiles, y_tiles, y_tiles, y_tiles, gates, mod, g)


def kernel(x, c, w_in, w_out, norm_mix, norm_ffn, w_ada, b_ada, lam_qk, diff_subln, sb_norm, rel_bias,
           w_router, b_router, w_gate_up, b_gate_up, w_down, b_down, final_norm):
    b, s, d = x.shape
    depth = w_in.shape[0]
    assert d == D_MODEL and s % ATT_TQ == 0
    n_tok = b * s

    mod_all = _ada_mod(c, w_ada, b_ada).reshape(depth, b, N_MOD, d)

    q_scale = HEAD_DIM ** -0.5 * LOG2E
    colscale = jnp.concatenate([
        jnp.full((DIFF_WIDTH,), q_scale, jnp.float32), jnp.ones((2 * DIFF_WIDTH,), jnp.float32),
        jnp.full((SB_WIDTH,), -q_scale, jnp.float32), jnp.ones((2 * SB_WIDTH,), jnp.float32)]).reshape(1, PROJ_WIDTH)
    bias_tiles, bias_top, bias_spread = _diff_bias_tiles(rel_bias)
    kk = jnp.arange(SB_SUB)
    tri = (kk[:, None] >= kk[None, :]).astype(jnp.bfloat16)

    xf = x.reshape(n_tok, d)
    for l in range(depth):
        mod = mod_all[l]
        lam_init = 0.8 - 0.6 * math.exp(-0.3 * l)
        lq = lam_qk[l].astype(jnp.float32)
        lam = jnp.exp(jnp.sum(lq[0] * lq[1])) - jnp.exp(jnp.sum(lq[2] * lq[3])) + lam_init
        aux = jnp.broadcast_to(jnp.stack([lam, bias_top, bias_spread] + [jnp.float32(0.0)] * (SUBLANES - 3))[:, None],
                               (SUBLANES, LANES)).astype(jnp.float32)

        proj = _inproj(xf.reshape(b, s, d), mod, norm_mix[l].reshape(1, d), w_in[l].astype(jnp.bfloat16), colscale)
        o_diff = _diff_attention(proj, bias_tiles, aux, diff_subln[l].reshape(1, LANES), 1.0 - lam_init)
        o_sb = _sb_attention(proj, tri, jnp.tile(sb_norm[l], 2).reshape(1, LANES))

        x1, h_tiles, route, picks = _outproj_router(
            xf, o_diff.reshape(n_tok, DIFF_WIDTH), o_sb.reshape(n_tok, SB_WIDTH), mod,
            norm_ffn[l].reshape(1, d), w_out[l].astype(jnp.bfloat16),
            w_router[l].T, b_router[l].reshape(N_EXPERTS, 1), s)

        block_e, slot_tok, slot_dst = _routing_tables(picks)
        y_tiles = _moe(h_tiles, block_e, slot_tok, slot_dst,
                       w_gate_up[l].astype(jnp.bfloat16), b_gate_up[l].reshape(N_EXPERTS, 1, 2 * D_FF),
                       w_down[l].astype(jnp.bfloat16), b_down[l].reshape(N_EXPERTS, 1, d))
        xf = _combine(x1, y_tiles, route, mod, final_norm.reshape(1, d), s, final=(l == depth - 1))
    return xf.reshape(b, s, d)
```

```python
import functools
import math

import jax
import jax.numpy as jnp
from jax import lax
from jax.experimental import pallas as pl
from jax.experimental.pallas import tpu as pltpu

D_MODEL = 1024
HEAD_DIM = 64
CHUNK = 64
DIFF_WIDTH = 512
N_DIFF_HEADS = 4
SB_WIDTH = 512
N_SB_HEADS = 8
PROJ_WIDTH = 3 * DIFF_WIDTH + 3 * SB_WIDTH
N_BUCKETS = 32
MAX_DISTANCE = 128
N_EXPERTS = 32
TOP_K = 4
D_FF = D_MODEL
SWIGLU_LIMIT = 7.0
SWIGLU_ALPHA = 1.702
MOE_BLOCK = 256
NORM_EPS = 1e-5
N_MOD = 6

LANES = 128
SUBLANES = 8
TOKEN_TILES = D_MODEL // LANES
N_COL_BLOCKS = PROJ_WIDTH // LANES
ATT_TQ = 512
ATT_TK = 512
SB_SUB = 256
SB_NSUB = 2
SB_DEAD_LOG2 = -192.0
GATHER_AHEAD = 2
ROW_TILE = 512
COMBINE_TILE = 256
MASK_VALUE = -1e30
LOG2E = math.log2(math.e)
VMEM_LIMIT = 56 * 1024 * 1024

_NT_DIMS = (((1,), (1,)), ((), ()))

assert TOKEN_TILES == SUBLANES and ATT_TQ == ATT_TK and ATT_TQ % (SB_SUB * SB_NSUB) == 0


def _cparams(sem):
    return pltpu.CompilerParams(dimension_semantics=sem, vmem_limit_bytes=VMEM_LIMIT)


def _ada_kernel(c_ref, w_ref, b_ref, o_ref):
    c = c_ref[...]
    cond = c * jax.nn.sigmoid(c)
    o_ref[0] = jnp.dot(cond, w_ref[0], preferred_element_type=jnp.float32,
                       precision=lax.Precision.HIGHEST) + b_ref[0]


def _ada_mod(c, w_ada, b_ada):
    depth, d, n = w_ada.shape
    b = c.shape[0]
    tn = 1536
    return pl.pallas_call(
        _ada_kernel,
        grid=(depth, n // tn),
        in_specs=[pl.BlockSpec((b, d), lambda l, j: (0, 0)),
                  pl.BlockSpec((1, d, tn), lambda l, j: (l, 0, j)),
                  pl.BlockSpec((1, 1, tn), lambda l, j: (l, 0, j))],
        out_specs=pl.BlockSpec((1, b, tn), lambda l, j: (l, 0, j)),
        out_shape=jax.ShapeDtypeStruct((depth, b, n), jnp.float32),
        compiler_params=_cparams(("arbitrary", "arbitrary")),
        name="ada_mod",
    )(c, w_ada, b_ada.reshape(depth, 1, n))


def _inproj_kernel(x_ref, mod_ref, g_ref, w_ref, cs_ref, o_ref):
    x = x_ref[0]
    ms = jnp.mean(x * x, axis=-1, keepdims=True)
    y = x * lax.rsqrt(ms + NORM_EPS) * g_ref[...]
    h = y * (1.0 + mod_ref[0, 1:2, :]) + mod_ref[0, 0:1, :]
    hb = h.astype(jnp.bfloat16)
    nchunk = 512
    for c in range(PROJ_WIDTH // nchunk):
        acc = jnp.dot(hb, w_ref[:, c * nchunk:(c + 1) * nchunk], preferred_element_type=jnp.float32)
        acc = acc * cs_ref[:, c * nchunk:(c + 1) * nchunk]
        for j in range(nchunk // LANES):
            o_ref[0, c * (nchunk // LANES) + j] = acc[:, j * LANES:(j + 1) * LANES].astype(jnp.bfloat16)


def _inproj(x, mod, g, w_bf, colscale):
    b, s, d = x.shape
    tm = min(ROW_TILE, s)
    return pl.pallas_call(
        _inproj_kernel,
        grid=(b, s // tm),
        in_specs=[pl.BlockSpec((1, tm, d), lambda bi, i: (bi, i, 0)),
                  pl.BlockSpec((1, N_MOD, d), lambda bi, i: (bi, 0, 0)),
                  pl.BlockSpec((1, d), lambda bi, i: (0, 0)),
                  pl.BlockSpec((d, PROJ_WIDTH), lambda bi, i: (0, 0)),
                  pl.BlockSpec((1, PROJ_WIDTH), lambda bi, i: (0, 0))],
        out_specs=pl.BlockSpec((1, N_COL_BLOCKS, tm, LANES), lambda bi, i: (bi, 0, i, 0)),
        out_shape=jax.ShapeDtypeStruct((b, N_COL_BLOCKS, s, LANES), jnp.bfloat16),
        compiler_params=_cparams(("arbitrary", "arbitrary")),
        name="inproj",
    )(x, mod, g, w_bf, colscale)


def _split_halves(q):
    lane = lax.broadcasted_iota(jnp.int32, q.shape, 1)
    zero = jnp.zeros_like(q)
    return jnp.concatenate([jnp.where(lane < HEAD_DIM, q, zero), jnp.where(lane >= HEAD_DIM, q, zero)], axis=0)


def _lane_tile(x, n):
    return x if n == 1 else jnp.concatenate([x] * n, axis=1)


DIFF_SHIFT_MARGIN = 1.01
DIFF_SHIFT_SPAN = 100.0


def _diff_kernel(q_ref, k_ref, v_ref, bias_ref, aux_ref, g_ref, o_ref, m_ref, l_ref, acc_ref, kn_ref, *, out_scale):
    i = pl.program_id(2)
    tq, tk = ATT_TQ, ATT_TK
    seq = k_ref.shape[2]
    q2 = _split_halves(q_ref[0, 0])
    first = lax.broadcasted_iota(jnp.int32, (tk, LANES), 1) < HEAD_DIM

    @pl.when(i == 0)
    def _():
        def chunk(c, mx):
            kc = k_ref[0, 0, pl.ds(pl.multiple_of(c * tk, tk), tk), :].astype(jnp.float32)
            sq = kc * kc
            lo = jnp.max(jnp.sum(jnp.where(first, sq, 0.0), axis=1, keepdims=True), axis=0, keepdims=True)
            hi = jnp.max(jnp.sum(jnp.where(first, 0.0, sq), axis=1, keepdims=True), axis=0, keepdims=True)
            return jnp.maximum(mx[0], lo), jnp.maximum(mx[1], hi)
        zero = jnp.zeros((1, 1), jnp.float32)
        lo, hi = lax.fori_loop(0, seq // tk, chunk, (zero, zero))
        kn_ref[0:1, :] = jnp.broadcast_to(jnp.sqrt(lo), (1, LANES))
        kn_ref[1:2, :] = jnp.broadcast_to(jnp.sqrt(hi), (1, LANES))

    qf = q2.astype(jnp.float32)
    qn = jnp.sqrt(jnp.sum(qf * qf, axis=1, keepdims=True))
    bound = jnp.concatenate([qn[:tq] * kn_ref[0:1, :], qn[tq:] * kn_ref[1:2, :]], axis=0) * DIFF_SHIFT_MARGIN
    fixed_ok = jnp.max(2.0 * bound + aux_ref[2:3, :]) <= DIFF_SHIFT_SPAN

    l_ref[...] = jnp.zeros(l_ref.shape, jnp.float32)
    acc_ref[...] = jnp.zeros(acc_ref.shape, jnp.float32)

    def scores(j, bias):
        k = k_ref[0, 0, pl.ds(pl.multiple_of(j * tk, tk), tk), :]
        s = lax.dot_general(q2, k, _NT_DIMS, preferred_element_type=jnp.float32)
        return s if bias is None else s + jnp.concatenate([bias, bias], axis=0)

    def sweep(update, unroll):
        update(i, bias_ref[0, :, tk:2 * tk])

        @pl.when(i >= 1)
        def _():
            update(i - 1, bias_ref[0, :, 0:tk])

        n_far = jnp.maximum(i - 1, 0)

        def far(t, carry):
            for u in range(unroll):
                update(unroll * t + u, None)
            return carry
        lax.fori_loop(0, n_far // unroll, far, 0)

        def rest(j, carry):
            update(j, None)
            return carry
        lax.fori_loop(n_far - n_far % unroll, n_far, rest, 0)

    @pl.when(fixed_ok)
    def _():
        m_ref[...] = bound + aux_ref[1:2, :]

        def update(j, bias):
            v = v_ref[0, 0, pl.ds(pl.multiple_of(j * tk, tk), tk), :]
            p = jnp.exp2(scores(j, bias) - _lane_tile(m_ref[...], tk // LANES))
            part = p[:, 0:LANES]
            for c in range(1, tk // LANES):
                part = part + p[:, c * LANES:(c + 1) * LANES]
            l_ref[...] += part
            acc_ref[...] += jnp.dot(p.astype(jnp.bfloat16), v, preferred_element_type=jnp.float32)

        sweep(update, 2)
        l_ref[...] = jnp.broadcast_to(jnp.sum(l_ref[...], axis=1, keepdims=True), l_ref.shape)

    @pl.when(jnp.logical_not(fixed_ok))
    def _():
        m_ref[...] = jnp.full(m_ref.shape, -jnp.inf, jnp.float32)

        def update(j, bias):
            v = v_ref[0, 0, pl.ds(pl.multiple_of(j * tk, tk), tk), :]
            s = scores(j, bias)
            m_prev = m_ref[...]
            m_next = jnp.maximum(m_prev, jnp.max(s, axis=1, keepdims=True))
            p = jnp.exp2(s - _lane_tile(m_next, tk // LANES))
            alpha = jnp.exp2(m_prev - m_next)
            l_ref[...] = alpha * l_ref[...] + jnp.sum(p, axis=1, keepdims=True)
            acc_ref[...] = alpha * acc_ref[...] + jnp.dot(p.astype(jnp.bfloat16), v,
                                                          preferred_element_type=jnp.float32)
            m_ref[...] = m_next

        sweep(update, 1)

    o = acc_ref[...] * (1.0 / l_ref[...])
    od = o[:tq] - aux_ref[0:1, :] * o[tq:]
    ms = jnp.mean(od * od, axis=-1, keepdims=True)
    y = od * lax.rsqrt(ms + NORM_EPS) * g_ref[...] * out_scale
    o_ref[0] = y.astype(jnp.bfloat16)


def _diff_attention(proj, bias, aux, subln, out_scale):
    b, _, s, _ = proj.shape
    tq = ATT_TQ
    kern = functools.partial(_diff_kernel, out_scale=out_scale)
    return pl.pallas_call(
        kern,
        grid=(b, N_DIFF_HEADS, s // tq),
        in_specs=[pl.BlockSpec((1, 1, tq, LANES), lambda bi, h, i: (bi, h, i, 0)),
                  pl.BlockSpec((1, 1, s, LANES), lambda bi, h, i: (bi, N_DIFF_HEADS + h, 0, 0)),
                  pl.BlockSpec((1, 1, s, LANES), lambda bi, h, i: (bi, 2 * N_DIFF_HEADS + h, 0, 0)),
                  pl.BlockSpec((1, tq, 2 * ATT_TK), lambda bi, h, i: (h, 0, 0)),
                  pl.BlockSpec((SUBLANES, LANES), lambda bi, h, i: (0, 0)),
                  pl.BlockSpec((1, LANES), lambda bi, h, i: (0, 0))],
        out_specs=pl.BlockSpec((1, tq, LANES), lambda bi, h, i: (bi, i, h)),
        out_shape=jax.ShapeDtypeStruct((b, s, DIFF_WIDTH), jnp.bfloat16),
        scratch_shapes=[pltpu.VMEM((2 * tq, LANES), jnp.float32),
                        pltpu.VMEM((2 * tq, LANES), jnp.float32),
                        pltpu.VMEM((2 * tq, LANES), jnp.float32),
                        pltpu.VMEM((SUBLANES, LANES), jnp.float32)],
        compiler_params=_cparams(("arbitrary", "arbitrary", "arbitrary")),
        name="diff_attn",
    )(proj, proj, proj, bias, aux, subln)


def _t5_bucket(rel):
    half = N_BUCKETS // 2
    max_exact = half // 2
    n = jnp.abs(rel)
    nf = jnp.maximum(n, 1).astype(jnp.float32)
    large = max_exact + (jnp.log(nf / max_exact) / math.log(MAX_DISTANCE / max_exact)
                         * (half - max_exact)).astype(jnp.int32)
    large = jnp.minimum(large, half - 1)
    return jnp.where(rel > 0, half, 0) + jnp.where(n < max_exact, n, large)


def _diff_bias_tiles(rel_bias):
    tq, tk = ATT_TQ, ATT_TK
    span = 2 * tk + tq
    rel = jnp.arange(span, dtype=jnp.int32) - (tk + tq - 1)
    far_bucket = _t5_bucket(jnp.full((1,), -(tk + 1), jnp.int32))[0]
    rb = rel_bias.astype(jnp.float32)
    by_rel = ((rb[_t5_bucket(rel)] - rb[far_bucket]) * LOG2E).T
    nh = by_rel.shape[0]
    shifted = jnp.tile(by_rel, (1, tq))[:, :tq * (span - 1)].reshape(nh, tq, span - 1)
    bias = shifted[:, :, tq - 1:tq - 1 + 2 * tk]
    qpos = jnp.arange(tq)[:, None]
    kpos = jnp.arange(-tk, tk)[None, :]
    allowed = (kpos // CHUNK) <= (qpos // CHUNK)
    top = jnp.maximum(jnp.max(by_rel), 0.0)
    spread = top - jnp.minimum(jnp.min(by_rel), 0.0)
    return jnp.where(allowed[None], bias, MASK_VALUE), top, spread


def _sb_kernel(q_ref, k_ref, v_ref, tri_ref, g_ref, o_ref, carry_ref, acc_ref):
    i = pl.program_id(2)
    tq, sub = ATT_TQ, SB_SUB
    q2 = _split_halves(q_ref[0, 0])
    tri = tri_ref[...]
    sign = jnp.uint32(0x80000000)

    carry_ref[...] = jnp.zeros(carry_ref.shape, jnp.float32)
    acc_ref[...] = jnp.zeros(acc_ref.shape, jnp.float32)

    def group(jb, masked, nsub):
        carry = carry_ref[...]
        a_parts, v_parts = [], []
        for t in range(nsub):
            off = pl.multiple_of((jb - t) * sub, sub)
            k = k_ref[0, 0, pl.ds(off, sub), :]
            v_parts.append(v_ref[0, 0, pl.ds(off, sub), :])
            u = lax.dot_general(q2, k, _NT_DIMS, preferred_element_type=jnp.float32)
            nabs = lax.bitcast_convert_type(lax.bitcast_convert_type(u, jnp.uint32) | sign, jnp.float32)
            lf = jnp.minimum(u, 0.0) - jnp.log2(1.0 + jnp.exp2(nabs))
            if masked:
                row = lax.broadcasted_iota(jnp.int32, (tq, sub), 0) + i * tq
                col = lax.broadcasted_iota(jnp.int32, (tq, sub), 1) + (jb - t) * sub
                strict = jnp.concatenate([col < row, col < row], axis=0)
                lf = jnp.where(strict, lf, 0.0)
            cum = jnp.dot(lf.astype(jnp.bfloat16), tri, preferred_element_type=jnp.float32)
            a = jnp.exp2(cum - u + _lane_tile(carry, sub // LANES))
            if masked:
                a = jnp.where(strict, a, 0.0)
            a_parts.append(a.astype(jnp.bfloat16))
            carry = carry + jnp.sum(lf, axis=1, keepdims=True)
        acc_ref[...] += jnp.dot(jnp.concatenate(a_parts, axis=1), jnp.concatenate(v_parts, axis=0),
                                preferred_element_type=jnp.float32)
        carry_ref[...] = carry

    nd = tq // sub
    for g in range(nd // SB_NSUB):
        group((i + 1) * nd - 1 - g * SB_NSUB, True, SB_NSUB)

    n_far = i * nd

    def alive(state):
        t, top = state
        return jnp.logical_and(t < n_far, top > SB_DEAD_LOG2)

    def far(state):
        t, _ = state
        group(i * nd - 1 - t, False, 1)
        return t + 1, jnp.max(carry_ref[...])

    lax.while_loop(alive, far, (jnp.int32(0), jnp.max(carry_ref[...])))

    acc = acc_ref[...]
    lane = lax.broadcasted_iota(jnp.int32, (tq, LANES), 1)
    first = lane < HEAD_DIM
    o = jnp.where(first, acc[:tq], acc[tq:])
    sq = o * o
    ss_a = jnp.sum(jnp.where(first, sq, 0.0), axis=-1, keepdims=True)
    ss_b = jnp.sum(jnp.where(first, 0.0, sq), axis=-1, keepdims=True)
    ms = jnp.where(first, ss_a, ss_b) * (1.0 / HEAD_DIM)
    o_ref[0] = (o * lax.rsqrt(ms + NORM_EPS) * g_ref[...]).astype(jnp.bfloat16)


def _sb_attention(proj, tri, gain2):
    b, _, s, _ = proj.shape
    tq = ATT_TQ
    npair = N_SB_HEADS // 2
    base = 3 * N_DIFF_HEADS
    return pl.pallas_call(
        _sb_kernel,
        grid=(b, npair, s // tq),
        in_specs=[pl.BlockSpec((1, 1, tq, LANES), lambda bi, h, i: (bi, base + h, i, 0)),
                  pl.BlockSpec((1, 1, s, LANES), lambda bi, h, i: (bi, base + npair + h, 0, 0)),
                  pl.BlockSpec((1, 1, s, LANES), lambda bi, h, i: (bi, base + 2 * npair + h, 0, 0)),
                  pl.BlockSpec((SB_SUB, SB_SUB), lambda bi, h, i: (0, 0)),
                  pl.BlockSpec((1, LANES), lambda bi, h, i: (0, 0))],
        out_specs=pl.BlockSpec((1, tq, LANES), lambda bi, h, i: (bi, i, h)),
        out_shape=jax.ShapeDtypeStruct((b, s, SB_WIDTH), jnp.bfloat16),
        scratch_shapes=[pltpu.VMEM((2 * tq, LANES), jnp.float32),
                        pltpu.VMEM((2 * tq, LANES), jnp.float32)],
        compiler_params=_cparams(("arbitrary", "arbitrary", "arbitrary")),
        name="sb_attn",
    )(proj, proj, proj, tri, gain2)


def _outproj_kernel(x_ref, od_ref, os_ref, mod_ref, g_ref, wo_ref, wr_ref, br_ref,
                    x1_ref, h_ref, route_ref, pick_ref):
    tm = x_ref.shape[0]
    y = jnp.dot(od_ref[...], wo_ref[0:DIFF_WIDTH, :], preferred_element_type=jnp.float32)
    y = y + jnp.dot(os_ref[...], wo_ref[DIFF_WIDTH:, :], preferred_element_type=jnp.float32)
    x1 = x_ref[...] + mod_ref[0, 2:3, :] * y
    x1_ref[...] = x1
    ms = jnp.mean(x1 * x1, axis=-1, keepdims=True)
    h = x1 * lax.rsqrt(ms + NORM_EPS) * g_ref[...]
    h = h * (1.0 + mod_ref[0, 4:5, :]) + mod_ref[0, 3:4, :]
    for c in range(TOKEN_TILES):
        h_ref[pl.ds(c, tm, stride=SUBLANES), :] = h[:, c * LANES:(c + 1) * LANES]
    logits = lax.dot_general(wr_ref[...], h, _NT_DIMS, preferred_element_type=jnp.float32,
                             precision=lax.Precision.HIGHEST) + br_ref[...]
    eidx = lax.broadcasted_iota(jnp.int32, logits.shape, 0)
    vals, idxs = [], []
    for _ in range(TOP_K):
        mx = jnp.max(logits, axis=0, keepdims=True)
        sel = jnp.min(jnp.where(logits == mx, eidx, N_EXPERTS), axis=0, keepdims=True)
        vals.append(mx)
        idxs.append(sel)
        logits = jnp.where(eidx == sel, -jnp.inf, logits)
    ex = [jnp.exp(v - vals[0]) for v in vals]
    inv = 1.0 / (ex[0] + ex[1] + ex[2] + ex[3])
    pick_ref[...] = jnp.concatenate(idxs, axis=0)
    packed = jnp.concatenate([e * inv for e in ex] + [jnp.zeros((LANES - TOP_K, tm), jnp.float32)], axis=0)
    route_ref[...] = packed.T


def _outproj_router(x2d, od, osb, mod, g, wo_bf, wr, br, seq):
    n, d = x2d.shape
    tm = min(ROW_TILE, seq)
    per_b = seq // tm
    return pl.pallas_call(
        _outproj_kernel,
        grid=(n // tm,),
        in_specs=[pl.BlockSpec((tm, d), lambda i: (i, 0)),
                  pl.BlockSpec((tm, DIFF_WIDTH), lambda i: (i, 0)),
                  pl.BlockSpec((tm, SB_WIDTH), lambda i: (i, 0)),
                  pl.BlockSpec((1, N_MOD, d), lambda i: (i // per_b, 0, 0)),
                  pl.BlockSpec((1, d), lambda i: (0, 0)),
                  pl.BlockSpec((d, d), lambda i: (0, 0)),
                  pl.BlockSpec((N_EXPERTS, d), lambda i: (0, 0)),
                  pl.BlockSpec((N_EXPERTS, 1), lambda i: (0, 0))],
        out_specs=[pl.BlockSpec((tm, d), lambda i: (i, 0)),
                   pl.BlockSpec((tm * SUBLANES, LANES), lambda i: (i, 0)),
                   pl.BlockSpec((tm, LANES), lambda i: (i, 0)),
                   pl.BlockSpec((TOP_K, tm), lambda i: (0, i))],
        out_shape=[jax.ShapeDtypeStruct((n, d), jnp.float32),
                   jax.ShapeDtypeStruct((n * SUBLANES, LANES), jnp.float32),
                   jax.ShapeDtypeStruct((n, LANES), jnp.float32),
                   jax.ShapeDtypeStruct((TOP_K, n), jnp.int32)],
        compiler_params=_cparams(("arbitrary",)),
        name="outproj_router",
    )(x2d, od, osb, mod, g, wo_bf, wr, br)


def _moe_n_blocks(n_tok):
    n_rows = n_tok * TOP_K
    return -(-(n_rows + N_EXPERTS * (MOE_BLOCK - 1)) // MOE_BLOCK)


N_SPARE = 2 * MOE_BLOCK


def _routing_tables(picks):
    n_tok = picks.shape[1]
    n_rows = n_tok * TOP_K
    n_blocks = _moe_n_blocks(n_tok)
    n_slots = n_blocks * MOE_BLOCK
    n_fill = MOE_BLOCK - 1
    stride = n_rows + MOE_BLOCK
    flat_e = picks.reshape(-1)
    experts = jnp.arange(N_EXPERTS, dtype=jnp.int32)
    counts = jnp.sum(flat_e[None, :] == experts[:, None], axis=1).astype(jnp.int32)
    need = (-counts) % MOE_BLOCK
    real = flat_e * stride + jnp.arange(n_rows, dtype=jnp.int32)
    j = jnp.arange(n_fill, dtype=jnp.int32)
    fill = jnp.where(j[None, :] < need[:, None], experts[:, None] * stride + n_rows + j[None, :],
                     N_EXPERTS * stride).reshape(-1)
    tail = jnp.full((n_slots - n_rows - N_EXPERTS * n_fill,), N_EXPERTS * stride, jnp.int32)
    skeys = lax.sort(jnp.concatenate([real, fill, tail])).reshape(n_blocks, MOE_BLOCK)
    block_e = jnp.minimum(skeys[:, 0] // stride, N_EXPERTS - 1).astype(jnp.int32)
    f = skeys % stride
    valid = jnp.logical_and(f < n_rows, skeys < N_EXPERTS * stride)
    slot = jnp.arange(n_slots, dtype=jnp.int32).reshape(n_blocks, MOE_BLOCK)
    slot_dst = jnp.where(valid, f, n_rows + slot % N_SPARE).astype(jnp.int32)
    slot_tok = jnp.where(valid, f % n_tok, 0).astype(jnp.int32)
    return block_e, slot_tok.reshape(n_blocks, 1, MOE_BLOCK), slot_dst.reshape(n_blocks, 1, MOE_BLOCK)


def _moe_kernel(be_ref, tok0_ref, tok1_ref, tokn_ref, dst_ref, h_hbm, wgu_ref, bgu_ref, wd_ref, bd_ref, y_hbm,
                xbuf, ybuf, gsem, ssem, *, n_rows):
    i = pl.program_id(0)
    nb = pl.num_programs(0)
    rows = MOE_BLOCK
    nx = GATHER_AHEAD + 1
    xslot = i % nx
    yslot = i % 2

    def start_gather(idx_ref, slot):
        for r in range(rows):
            pltpu.make_async_copy(h_hbm.at[pl.ds(idx_ref[0, 0, r] * SUBLANES, SUBLANES)],
                                  xbuf.at[slot, pl.ds(r * SUBLANES, SUBLANES)], gsem.at[slot]).start(priority=r % 2)

    def wait_gather(slot):
        pltpu.make_async_copy(h_hbm.at[pl.ds(0, rows * SUBLANES)], xbuf.at[slot], gsem.at[slot]).wait()

    def wait_scatter(slot):
        pltpu.make_async_copy(ybuf.at[slot], y_hbm.at[pl.ds(0, rows * SUBLANES)], ssem.at[slot]).wait()

    @pl.when(i == 0)
    def _():
        start_gather(tok0_ref, 0)
        start_gather(tok1_ref, 1)
        ybuf[1] = jnp.zeros(ybuf.shape[1:], jnp.float32)
        for part in range(N_SPARE // rows):
            pltpu.make_async_copy(ybuf.at[1], y_hbm.at[pl.ds((n_rows + part * rows) * SUBLANES, rows * SUBLANES)],
                                  ssem.at[1]).start()
        for part in range(N_SPARE // rows):
            wait_scatter(1)

    wait_gather(xslot)
    xb = jnp.concatenate([xbuf[xslot, pl.ds(c, rows, stride=SUBLANES), :] for c in range(TOKEN_TILES)],
                         axis=1).astype(jnp.bfloat16)
    gu = jnp.dot(xb, wgu_ref[0, 0], preferred_element_type=jnp.float32) + bgu_ref[0]
    glu = jnp.minimum(gu[:, :D_FF], SWIGLU_LIMIT)
    lin = jnp.clip(gu[:, D_FF:], -SWIGLU_LIMIT, SWIGLU_LIMIT)
    act = glu * jax.nn.sigmoid(SWIGLU_ALPHA * glu) * (lin + 1.0)
    y = jnp.dot(act.astype(jnp.bfloat16), wd_ref[0, 0], preferred_element_type=jnp.float32) + bd_ref[0]

    start_gather(tokn_ref, (i + GATHER_AHEAD) % nx)

    @pl.when(i >= 2)
    def _():
        wait_scatter(yslot)

    for c in range(TOKEN_TILES):
        ybuf[yslot, pl.ds(c, rows, stride=SUBLANES), :] = y[:, c * LANES:(c + 1) * LANES]
    for r in range(rows):
        pltpu.make_async_copy(ybuf.at[yslot, pl.ds(r * SUBLANES, SUBLANES)],
                              y_hbm.at[pl.ds(dst_ref[0, 0, r] * SUBLANES, SUBLANES)],
                              ssem.at[yslot]).start(priority=r % 2)

    @pl.when(i == nb - 1)
    def _():
        wait_gather((i + 1) % nx)
        wait_gather((i + 2) % nx)
        wait_scatter(yslot)

        @pl.when(nb >= 2)
        def _():
            wait_scatter(1 - yslot)


def _moe(h_tiles, block_e, slot_tok, slot_dst, wgu_bf, bgu, wd_bf, bd, layer):
    n_tok = h_tiles.shape[0] // SUBLANES
    d = D_MODEL
    n_rows = n_tok * TOP_K
    n_blocks = slot_tok.shape[0]
    rows = MOE_BLOCK
    assert n_blocks >= GATHER_AHEAD

    def idx_block(ahead):
        return pl.BlockSpec((1, 1, rows), lambda i, be: (jnp.minimum(i + ahead, n_blocks - 1), 0, 0),
                            memory_space=pltpu.SMEM)

    grid_spec = pltpu.PrefetchScalarGridSpec(
        num_scalar_prefetch=1,
        grid=(n_blocks,),
        in_specs=[idx_block(0), idx_block(1), idx_block(GATHER_AHEAD), idx_block(0),
                  pl.BlockSpec(memory_space=pl.ANY),
                  pl.BlockSpec((1, 1, d, 2 * D_FF), lambda i, be: (layer, be[i], 0, 0)),
                  pl.BlockSpec((1, 1, 2 * D_FF), lambda i, be: (be[i], 0, 0)),
                  pl.BlockSpec((1, 1, D_FF, d), lambda i, be: (layer, be[i], 0, 0)),
                  pl.BlockSpec((1, 1, d), lambda i, be: (be[i], 0, 0))],
        out_specs=pl.BlockSpec(memory_space=pl.ANY),
        scratch_shapes=[pltpu.VMEM((GATHER_AHEAD + 1, rows * SUBLANES, LANES), jnp.float32),
                        pltpu.VMEM((2, rows * SUBLANES, LANES), jnp.float32),
                        pltpu.SemaphoreType.DMA((GATHER_AHEAD + 1,)),
                        pltpu.SemaphoreType.DMA((2,))],
    )
    return pl.pallas_call(
        functools.partial(_moe_kernel, n_rows=n_rows),
        grid_spec=grid_spec,
        out_shape=jax.ShapeDtypeStruct(((n_rows + N_SPARE) * SUBLANES, LANES), jnp.float32),
        compiler_params=_cparams(("arbitrary",)),
        name="moe_experts",
    )(block_e, slot_tok, slot_tok, slot_tok, slot_dst, h_tiles, wgu_bf, bgu, wd_bf, bd)


def _combine_kernel(x_ref, y0_ref, y1_ref, y2_ref, y3_ref, gate_ref, mod_ref, g_ref, o_ref, *, final):
    tm = x_ref.shape[0]
    y_refs = (y0_ref, y1_ref, y2_ref, y3_ref)
    gk = [jnp.broadcast_to(gate_ref[:, k:k + 1], (tm, LANES)) for k in range(TOP_K)]
    parts = []
    for c in range(TOKEN_TILES):
        acc = gk[0] * y_refs[0][pl.ds(c, tm, stride=SUBLANES), :]
        for k in range(1, TOP_K):
            acc = acc + gk[k] * y_refs[k][pl.ds(c, tm, stride=SUBLANES), :]
        parts.append(acc)
    x2 = x_ref[...] + mod_ref[0, 5:6, :] * jnp.concatenate(parts, axis=1)
    if final:
        ms = jnp.mean(x2 * x2, axis=-1, keepdims=True)
        x2 = x2 * lax.rsqrt(ms + NORM_EPS) * g_ref[...]
    o_ref[...] = x2


def _combine(x1, y_tiles, gates, mod, g, seq, final):
    n, d = x1.shape
    tm = min(COMBINE_TILE, seq)
    per_b = seq // tm
    plane_blocks = n // tm
    assert n % tm == 0 and TOP_K == 4

    def plane_spec(k):
        return pl.BlockSpec((tm * SUBLANES, LANES), lambda i: (k * plane_blocks + i, 0))

    return pl.pallas_call(
        functools.partial(_combine_kernel, final=final),
        grid=(n // tm,),
        in_specs=[pl.BlockSpec((tm, d), lambda i: (i, 0)),
                  plane_spec(0), plane_spec(1), plane_spec(2), plane_spec(3),
                  pl.BlockSpec((tm, LANES), lambda i: (i, 0)),
                  pl.BlockSpec((1, N_MOD, d), lambda i: (i // per_b, 0, 0)),
                  pl.BlockSpec((1, d), lambda i: (0, 0))],
        out_specs=pl.BlockSpec((tm, d), lambda i: (i, 0)),
        out_shape=jax.ShapeDtypeStruct((n, d), jnp.float32),
        compiler_params=_cparams(("arbitrary",)),
        name="combine",
    )(x1, y_tiles, y_tiles, y_tiles, y_tiles, gates, mod, g)


def kernel(x, c, w_in, w_out, norm_mix, norm_ffn, w_ada, b_ada, lam_qk, diff_subln, sb_norm, rel_bias,
           w_router, b_router, w_gate_up, b_gate_up, w_down, b_down, final_norm):
    b, s, d = x.shape
    depth = w_in.shape[0]
    assert d == D_MODEL and s % ATT_TQ == 0
    n_tok = b * s

    mod_all = _ada_mod(c, w_ada, b_ada).reshape(depth, b, N_MOD, d)

    q_scale = HEAD_DIM ** -0.5 * LOG2E
    colscale = jnp.concatenate([
        jnp.full((DIFF_WIDTH,), q_scale, jnp.float32), jnp.ones((2 * DIFF_WIDTH,), jnp.float32),
        jnp.full((SB_WIDTH,), -q_scale, jnp.float32), jnp.ones((2 * SB_WIDTH,), jnp.float32)]).reshape(1, PROJ_WIDTH)
    bias_tiles, bias_top, bias_spread = _diff_bias_tiles(rel_bias)
    kk = jnp.arange(SB_SUB)
    tri = (kk[:, None] >= kk[None, :]).astype(jnp.bfloat16)

    w_gate_up_bf = w_gate_up.astype(jnp.bfloat16)
    w_down_bf = w_down.astype(jnp.bfloat16)

    xf = x.reshape(n_tok, d)
    for l in range(depth):
        mod = mod_all[l]
        lam_init = 0.8 - 0.6 * math.exp(-0.3 * l)
        lq = lam_qk[l].astype(jnp.float32)
        lam = jnp.exp(jnp.sum(lq[0] * lq[1])) - jnp.exp(jnp.sum(lq[2] * lq[3])) + lam_init
        aux = jnp.broadcast_to(jnp.stack([lam, bias_top, bias_spread] + [jnp.float32(0.0)] * (SUBLANES - 3))[:, None],
                               (SUBLANES, LANES)).astype(jnp.float32)

        proj = _inproj(xf.reshape(b, s, d), mod, norm_mix[l].reshape(1, d), w_in[l].astype(jnp.bfloat16), colscale)
        o_diff = _diff_attention(proj, bias_tiles, aux, diff_subln[l].reshape(1, LANES), 1.0 - lam_init)
        o_sb = _sb_attention(proj, tri, jnp.tile(sb_norm[l], 2).reshape(1, LANES))

        x1, h_tiles, route, picks = _outproj_router(
            xf, o_diff.reshape(n_tok, DIFF_WIDTH), o_sb.reshape(n_tok, SB_WIDTH), mod,
            norm_ffn[l].reshape(1, d), w_out[l].astype(jnp.bfloat16),
            w_router[l].T, b_router[l].reshape(N_EXPERTS, 1), s)

        block_e, slot_tok, slot_dst = _routing_tables(picks)
        y_tiles = _moe(h_tiles, block_e, slot_tok, slot_dst,
                       w_gate_up_bf, b_gate_up[l].reshape(N_EXPERTS, 1, 2 * D_FF),
                       w_down_bf, b_down[l].reshape(N_EXPERTS, 1, d), l)
        xf = _combine(x1, y_tiles, route, mod, final_norm.reshape(1, d), s, final=(l == depth - 1))
    return xf.reshape(b, s, d)
```
